```python
import math
import jax
import jax.numpy as jnp
from jax import lax
import numpy as np

D_MODEL = 1024
BATCH = 16
SEQ = 2048
DEPTH = 2

GRID_W = 64
CTX_LEN = 256
N_HEADS = 8
HEAD_DIM = 64
V_HEAD_DIM = 2 * HEAD_DIM
ATTN_WIDTH = N_HEADS * V_HEAD_DIM
N_FOURIER_GROUPS = 4
FOURIER_GROUP_DIM = 128
FOURIER_WIDTH = N_FOURIER_GROUPS * FOURIER_GROUP_DIM
N_BRANCHES = 2
KV_COLS = 2 * ATTN_WIDTH
IN_WIDTH = 3 * ATTN_WIDTH + FOURIER_WIDTH + N_BRANCHES * D_MODEL
D_FF = -(-8 * D_MODEL // (3 * 256)) * 256
N_MOD = 6
ROPE_THETA = 10000.0
Q_BLOCK = 128
EPS = 1e-6

kernel_name = "hybrid_diffattn_fourier_dit"


def rmsnorm(x, g):
    xf = x.astype(jnp.float32)
    y = xf * lax.rsqrt(jnp.mean(xf * xf, axis=-1, keepdims=True) + EPS)
    return (y * g.astype(jnp.float32)).astype(x.dtype)


def modulate(h, shift, scale):
    return h * (1 + scale) + shift


def axial_rope_tables(n_tokens):
    rows = n_tokens // GRID_W
    r, col = jnp.meshgrid(jnp.arange(rows), jnp.arange(GRID_W), indexing="ij")
    r = r.reshape(-1).astype(jnp.float32)
    col = col.reshape(-1).astype(jnp.float32)
    axis_dim = HEAD_DIM // 2
    inv_freq = ROPE_THETA ** (-jnp.arange(0, axis_dim, 2, dtype=jnp.float32) / axis_dim)
    ang_r = r[:, None] * inv_freq[None, :]
    ang_c = col[:, None] * inv_freq[None, :]
    ang = jnp.concatenate([ang_r, ang_r, ang_c, ang_c], axis=-1)
    return jnp.cos(ang), jnp.sin(ang)


def rotate_half_axial(x):
    shp = x.shape
    xs = x.reshape(shp[:-1] + (2, 2, HEAD_DIM // 4))
    return jnp.stack([-xs[..., 1, :], xs[..., 0, :]], axis=-2).reshape(shp)


def apply_rope(x, cos, sin):
    cos = cos[None, :, None, None, :].astype(x.dtype)
    sin = sin[None, :, None, None, :].astype(x.dtype)
    return x * cos + rotate_half_axial(x) * sin


def diff_attention(q, k, v, lam):
    s = jnp.einsum("bqhid,bkhid->bhiqk", q.astype(jnp.float32), k.astype(jnp.float32)) * (HEAD_DIM ** -0.5)
    p = jax.nn.softmax(s, axis=-1)
    a = p[:, :, 0] - lam * p[:, :, 1]
    return jnp.einsum("bhqk,bkhe->bqhe", a, v.astype(jnp.float32)).astype(v.dtype)


def fourier_mix(f):
    b, t, _ = f.shape
    fg = f.reshape(b, t, N_FOURIER_GROUPS, FOURIER_GROUP_DIM).astype(jnp.float32)
    out = jnp.fft.fft2(fg, axes=(1, 3), norm="ortho").real
    return out.reshape(b, t, FOURIER_WIDTH).astype(f.dtype)


def swiglu(h, w_gate_up, w_down):
    gu = h @ w_gate_up
    return (jax.nn.silu(gu[..., :D_FF]) * gu[..., D_FF:]) @ w_down


def trunk_layer(x, cx, mod_lat, mod_ctx, cos, sin, lp, lam_init, ctx_out):
    b, t, _ = x.shape
    tc = cx.shape[1]
    sh1, sc1, gt1, sh2, sc2, gt2 = jnp.split(mod_lat, N_MOD, axis=-1)
    csh1, csc1, cgt1, csh2, csc2, cgt2 = jnp.split(mod_ctx, N_MOD, axis=-1)
    f32 = jnp.float32
    lam = (jnp.exp(jnp.sum(lp["lambda_q1"].astype(f32) * lp["lambda_k1"].astype(f32)))
           - jnp.exp(jnp.sum(lp["lambda_q2"].astype(f32) * lp["lambda_k2"].astype(f32))) + lam_init)

    def kv_heads(p, n):
        k = rmsnorm(p[..., :ATTN_WIDTH].reshape(b, n, N_HEADS, 2, HEAD_DIM), lp["k_norm_g"])
        v = p[..., ATTN_WIDTH:KV_COLS].reshape(b, n, N_HEADS, V_HEAD_DIM)
        return k, v

    def q_heads(p, n):
        return rmsnorm(p[..., KV_COLS:KV_COLS + ATTN_WIDTH].reshape(b, n, N_HEADS, 2, HEAD_DIM), lp["q_norm_g"])

    def merged_branches(attn_o, p_rest):
        n = attn_o.shape[1]
        o = rmsnorm(attn_o, lp["subln_g"]) * (1 - lam_init)
        y_a = o.reshape(b, n, ATTN_WIDTH) @ lp["w_proj_attn"]
        y_f = fourier_mix(p_rest[..., :FOURIER_WIDTH]) @ lp["w_proj_fourier"]
        g_a = jax.nn.sigmoid(p_rest[..., FOURIER_WIDTH:FOURIER_WIDTH + D_MODEL])
        g_f = jax.nn.sigmoid(p_rest[..., FOURIER_WIDTH + D_MODEL:])
        return (g_a * y_a + g_f * y_f) @ lp["w_out"]

    w_in = lp["w_in"]
    h = modulate(rmsnorm(x, lp["norm1_g"]), sh1, sc1)
    hc = modulate(rmsnorm(cx, lp["norm1_g"]), csh1, csc1)
    p = h @ w_in
    pc = hc @ (w_in if ctx_out else w_in[:, :KV_COLS])

    k, v = kv_heads(p, t)
    kc, vc = kv_heads(pc, tc)
    q = apply_rope(q_heads(p, t), cos, sin)
    k = apply_rope(k, cos, sin)
    k_all = jnp.concatenate([kc, k], axis=1)
    v_all = jnp.concatenate([vc, v], axis=1)
    n_blocks = t // Q_BLOCK
    qb = q.reshape(b, n_blocks, Q_BLOCK, N_HEADS, 2, HEAD_DIM).swapaxes(0, 1)
    ob = lax.map(lambda qi: diff_attention(qi, k_all, v_all, lam), qb)
    o = ob.swapaxes(0, 1).reshape(b, t, N_HEADS, V_HEAD_DIM)
    x = x + gt1 * merged_branches(o, p[..., KV_COLS + ATTN_WIDTH:])

    h2 = modulate(rmsnorm(x, lp["norm2_g"]), sh2, sc2)
    x = x + gt2 * swiglu(h2, lp["w_gate_up"], lp["w_down"])

    if not ctx_out:
        return x, cx

    oc = diff_attention(q_heads(pc, tc), kc, vc, lam)
    cx = cx + cgt1 * merged_branches(oc, pc[..., KV_COLS + ATTN_WIDTH:])
    hc2 = modulate(rmsnorm(cx, lp["norm2_g"]), csh2, csc2)
    cx = cx + cgt2 * swiglu(hc2, lp["w_gate_up"], lp["w_down"])
    return x, cx


def setup_inputs(seed: int = 0) -> dict:
    key = jax.random.key(seed)
    ks = jax.random.split(key, 21)

    def nrm(k, shape, scale):
        return jax.random.normal(k, shape, jnp.float32) * scale

    def gain(k, shape):
        return 1.0 + 0.02 * jax.random.normal(k, shape, jnp.float32)

    return {
        "x": nrm(ks[0], (BATCH, SEQ, D_MODEL), 1.0),
        "c": nrm(ks[1], (BATCH, D_MODEL), 1.0),
        "ctx": nrm(ks[2], (BATCH, CTX_LEN, D_MODEL), 1.0),
        "c_ctx": nrm(ks[3], (D_MODEL,), 1.0),
        "w_ada": nrm(ks[4], (DEPTH, D_MODEL, N_MOD * D_MODEL), D_MODEL ** -0.5),
        "b_ada": nrm(ks[5], (DEPTH, N_MOD * D_MODEL), 0.01),
        "norm1_g": gain(ks[6], (DEPTH, D_MODEL)),
        "norm2_g": gain(ks[7], (DEPTH, D_MODEL)),
        "w_in": nrm(ks[8], (DEPTH, D_MODEL, IN_WIDTH), D_MODEL ** -0.5),
        "q_norm_g": gain(ks[9], (DEPTH, HEAD_DIM)),
        "k_norm_g": gain(ks[10], (DEPTH, HEAD_DIM)),
        "lambda_q1": nrm(ks[11], (DEPTH, HEAD_DIM), 0.1),
        "lambda_k1": nrm(ks[12], (DEPTH, HEAD_DIM), 0.1),
        "lambda_q2": nrm(ks[13], (DEPTH, HEAD_DIM), 0.1),
        "lambda_k2": nrm(ks[14], (DEPTH, HEAD_DIM), 0.1),
        "subln_g": gain(ks[15], (DEPTH, V_HEAD_DIM)),
        "w_proj_attn": nrm(ks[16], (DEPTH, ATTN_WIDTH, D_MODEL), ATTN_WIDTH ** -0.5),
        "w_proj_fourier": nrm(ks[17], (DEPTH, FOURIER_WIDTH, D_MODEL), FOURIER_WIDTH ** -0.5),
        "w_out": nrm(ks[18], (DEPTH, D_MODEL, D_MODEL), D_MODEL ** -0.5),
        "w_gate_up": nrm(ks[19], (DEPTH, D_MODEL, 2 * D_FF), D_MODEL ** -0.5),
        "w_down": nrm(ks[20], (DEPTH, D_FF, D_MODEL), D_FF ** -0.5),
    }


def reference(x, c, ctx, c_ctx, w_ada, b_ada, norm1_g, norm2_g, w_in, q_norm_g, k_norm_g,
              lambda_q1, lambda_k1, lambda_q2, lambda_k2, subln_g, w_proj_attn, w_proj_fourier,
              w_out, w_gate_up, w_down):
    cos, sin = axial_rope_tables(x.shape[1])
    silu_c = jax.nn.silu(c)
    silu_cc = jax.nn.silu(c_ctx)
    cx = ctx
    for l in range(DEPTH):
        lp = {
            "norm1_g": norm1_g[l], "norm2_g": norm2_g[l], "w_in": w_in[l],
            "q_norm_g": q_norm_g[l], "k_norm_g": k_norm_g[l],
            "lambda_q1": lambda_q1[l], "lambda_k1": lambda_k1[l],
            "lambda_q2": lambda_q2[l], "lambda_k2": lambda_k2[l],
            "subln_g": subln_g[l], "w_proj_attn": w_proj_attn[l],
            "w_proj_fourier": w_proj_fourier[l], "w_out": w_out[l],
            "w_gate_up": w_gate_up[l], "w_down": w_down[l],
        }
        mod_lat = (silu_c @ w_ada[l] + b_ada[l])[:, None, :]
        mod_ctx = (silu_cc @ w_ada[l] + b_ada[l])[None, None, :]
        lam_init = 0.8 - 0.6 * math.exp(-0.3 * l)
        x, cx = trunk_layer(x, cx, mod_lat, mod_ctx, cos, sin, lp, lam_init, l < DEPTH - 1)
    return x
```

```python
import functools
import math

import jax
import jax.numpy as jnp
import numpy as np
from jax import lax
from jax.experimental import pallas as pl
from jax.experimental.pallas import tpu as pltpu

F32 = jnp.float32
BF16 = jnp.bfloat16

D_MODEL = 1024
N_HEADS = 8
HEAD_DIM = 64
V_HEAD_DIM = 2 * HEAD_DIM
ATTN_WIDTH = N_HEADS * V_HEAD_DIM
N_FOURIER_GROUPS = 4
FOURIER_GROUP_DIM = 128
FOURIER_WIDTH = N_FOURIER_GROUPS * FOURIER_GROUP_DIM
KV_COLS = 2 * ATTN_WIDTH
Q_COL0 = KV_COLS
F_COL0 = 3 * ATTN_WIDTH
G_COL0 = F_COL0 + FOURIER_WIDTH
IN_WIDTH = G_COL0 + 2 * D_MODEL
D_FF = 2816
N_MOD = 6
GRID_W = 64
ROPE_THETA = 10000.0
EPS = 1e-6

LANES = 128
MXU_DIM = 256
ONES_ROWS = 16
VT_ROWS = V_HEAD_DIM + ONES_ROWS
MOD_ROWS = 24
VMEM_LIMIT = 56 * 1024 * 1024
LOG2E = 1.4426950408889634


def _params(n_axes):
    return pltpu.CompilerParams(dimension_semantics=("arbitrary",) * n_axes,
                                vmem_limit_bytes=VMEM_LIMIT)


def _const_spec(shape):
    nd = len(shape)
    return pl.BlockSpec(shape, lambda *_: (0,) * nd, pipeline_mode=pl.Buffered(1))


def _dot(a, b):
    return jnp.dot(a, b, preferred_element_type=F32)


def _sigmoid(x):
    return 1.0 / (1.0 + jnp.exp(-x))


def _mod_kernel(c_ref, w_ref, b_ref, o_ref):
    c = c_ref[...]
    s = c * _sigmoid(c)
    s_hi = s.astype(BF16)
    s_lo = (s - s_hi.astype(F32)).astype(BF16)
    w = w_ref[0]
    w_hi = w.astype(BF16)
    w_lo = (w - w_hi.astype(F32)).astype(BF16)
    acc = _dot(s_hi, w_hi) + _dot(s_hi, w_lo) + _dot(s_lo, w_hi)
    o_ref[0] = acc + b_ref[0]


def _modulation(cc, w_ada, b_ada):
    depth, d, n = w_ada.shape
    tn = 1024
    return pl.pallas_call(
        _mod_kernel,
        grid=(depth, n // tn),
        in_specs=[
            pl.BlockSpec((MOD_ROWS, d), lambda l, j: (0, 0)),
            pl.BlockSpec((1, d, tn), lambda l, j: (l, 0, j)),
            pl.BlockSpec((1, 1, tn), lambda l, j: (l, 0, j)),
        ],
        out_specs=pl.BlockSpec((1, MOD_ROWS, tn), lambda l, j: (l, 0, j)),
        out_shape=jax.ShapeDtypeStruct((depth, MOD_ROWS, n), F32),
        compiler_params=_params(2),
        name="adaln_modulation",
    )(cc, w_ada, b_ada.reshape(depth, 1, n))


def _norm_modulate(x, g, shift, scale):
    ms = jnp.mean(x * x, axis=-1, keepdims=True)
    return (x * lax.rsqrt(ms + EPS) * g) * (1.0 + scale) + shift


def _inproj_kernel(x_ref, mod_ref, g1_ref, w_ref, cos_ref, sin_ref, qg_ref, kg_ref, bd_ref,
                   *out_refs, kv_only):
    if kv_only:
        k_ref, vT_ref = out_refs
    else:
        qT_ref, k_ref, vT_ref, f_ref, gate_ref = out_refs
    tm = x_ref.shape[1]
    mod = mod_ref[0]
    h = _norm_modulate(x_ref[0], g1_ref[...], mod[:, 0:D_MODEL], mod[:, D_MODEL:2 * D_MODEL])
    hb = h.astype(BF16)
    cos = cos_ref[...]
    sin = sin_ref[...]
    lane = lax.broadcasted_iota(jnp.int32, (tm, LANES), 1)
    first_half = jnp.bitwise_and(lane, 31) < 16
    bd = bd_ref[...]

    def normed_rope(col0, gain):
        p = _dot(hb, w_ref[:, col0:col0 + MXU_DIM])
        ssq = _dot((p * p).astype(BF16), bd)
        y = p * lax.rsqrt(ssq * (1.0 / HEAD_DIM) + EPS)
        slabs = []
        for s in range(MXU_DIM // LANES):
            ys = y[:, s * LANES:(s + 1) * LANES] * gain
            rot = jnp.where(first_half, pltpu.roll(ys, LANES - 16, 1), pltpu.roll(ys, 16, 1))
            slabs.append(ys * cos + rot * sin)
        return slabs

    kg = kg_ref[...]
    for c in range(ATTN_WIDTH // MXU_DIM):
        slabs = normed_rope(c * MXU_DIM, kg)
        for s, slab in enumerate(slabs):
            k_ref[0, 2 * c + s] = slab.astype(BF16)

    ones = jnp.ones((ONES_ROWS, tm), BF16)
    for c in range(ATTN_WIDTH // MXU_DIM):
        v = _dot(hb, w_ref[:, ATTN_WIDTH + c * MXU_DIM:ATTN_WIDTH + (c + 1) * MXU_DIM])
        for s in range(MXU_DIM // LANES):
            hd = 2 * c + s
            vT_ref[0, hd, 0:V_HEAD_DIM, :] = v[:, s * LANES:(s + 1) * LANES].T.astype(BF16)
            vT_ref[0, hd, V_HEAD_DIM:VT_ROWS, :] = ones

    if kv_only:
        return

    qg = qg_ref[...] * (HEAD_DIM ** -0.5 * LOG2E)
    for c in range(ATTN_WIDTH // MXU_DIM):
        slabs = normed_rope(Q_COL0 + c * MXU_DIM, qg)
        for s, slab in enumerate(slabs):
            st = slab.T.astype(BF16)
            for j in range(tm // MXU_DIM):
                qT_ref[0, 2 * c + s, j] = st[:, j * MXU_DIM:(j + 1) * MXU_DIM]

    f_ref[0] = _dot(hb, w_ref[:, F_COL0:F_COL0 + FOURIER_WIDTH]).astype(BF16)

    gw = 512
    for c in range(2 * D_MODEL // gw):
        g = _dot(hb, w_ref[:, G_COL0 + c * gw:G_COL0 + (c + 1) * gw])
        gate_ref[0, :, c * gw:(c + 1) * gw] = _sigmoid(g).astype(BF16)


def _inproj(x, mod, per_batch_mod, g1, w, cos, sin, qg, kg, bd, *, tm, kv_only):
    b, t, d = x.shape
    nq = tm // MXU_DIM
    n_w = w.shape[1]
    mod_map = (lambda i, j: (i, 0, 0)) if per_batch_mod else (lambda i, j: (0, 0, 0))
    in_specs = [
        pl.BlockSpec((1, tm, d), lambda i, j: (i, j, 0)),
        pl.BlockSpec((1, 1, N_MOD * d), mod_map),
        _const_spec((1, d)),
        _const_spec((d, n_w)),
        pl.BlockSpec((tm, LANES), lambda i, j: (j, 0)),
        pl.BlockSpec((tm, LANES), lambda i, j: (j, 0)),
        _const_spec((1, LANES)),
        _const_spec((1, LANES)),
        _const_spec((MXU_DIM, MXU_DIM)),
    ]
    k_spec = pl.BlockSpec((1, N_HEADS, tm, LANES), lambda i, j: (i, 0, j, 0))
    vT_spec = pl.BlockSpec((1, N_HEADS, VT_ROWS, tm), lambda i, j: (i, 0, 0, j))
    k_shape = jax.ShapeDtypeStruct((b, N_HEADS, t, LANES), BF16)
    vT_shape = jax.ShapeDtypeStruct((b, N_HEADS, VT_ROWS, t), BF16)
    if kv_only:
        out_specs = [k_spec, vT_spec]
        out_shape = [k_shape, vT_shape]
    else:
        out_specs = [
            pl.BlockSpec((1, N_HEADS, nq, LANES, MXU_DIM), lambda i, j: (i, 0, j, 0, 0)),
            k_spec, vT_spec,
            pl.BlockSpec((1, tm, FOURIER_WIDTH), lambda i, j: (i, j, 0)),
            pl.BlockSpec((1, tm, 2 * d), lambda i, j: (i, j, 0)),
        ]
        out_shape = [
            jax.ShapeDtypeStruct((b, N_HEADS, t // MXU_DIM, LANES, MXU_DIM), BF16),
            k_shape, vT_shape,
            jax.ShapeDtypeStruct((b, t, FOURIER_WIDTH), BF16),
            jax.ShapeDtypeStruct((b, t, 2 * d), BF16),
        ]
    return pl.pallas_call(
        functools.partial(_inproj_kernel, kv_only=kv_only),
        grid=(b, t // tm),
        in_specs=in_specs,
        out_specs=out_specs,
        out_shape=out_shape,
        compiler_params=_params(2),
        name="inproj_kv" if kv_only else "inproj",
    )(x, mod, g1, w, cos, sin, qg, kg, bd)


def _attn_kernel(*refs, has_lat, lam_init):
    if has_lat:
        qT_ref, kl_ref, kc_ref, vTl_ref, vTc_ref, lam_ref, sg_ref, o_ref, s_ref, p_ref = refs
        tl = kl_ref.shape[2]
    else:
        qT_ref, kc_ref, vTc_ref, lam_ref, sg_ref, o_ref, s_ref, p_ref = refs
        tl = 0
    n_blk = qT_ref.shape[2]
    lp = lam_ref[...]
    lam = (jnp.exp(jnp.sum(lp[0:1] * lp[1:2], axis=-1, keepdims=True))
           - jnp.exp(jnp.sum(lp[2:3] * lp[3:4], axis=-1, keepdims=True)) + lam_init)
    out_gain = sg_ref[...] * (1.0 - lam_init)
    row = lax.broadcasted_iota(jnp.int32, (V_HEAD_DIM, MXU_DIM), 0)
    zero = jnp.zeros((V_HEAD_DIM, MXU_DIM), BF16)

    def softmax_v(w):
        if has_lat:
            s_ref[0:tl] = _dot(kl_ref[0, 0], w)
        s_ref[tl:] = _dot(kc_ref[0, 0], w)
        s = s_ref[...]
        m = jnp.max(s, axis=0, keepdims=True)
        p_ref[...] = jnp.exp2(s - m).astype(BF16)
        acc = _dot(vTc_ref[0, 0], p_ref[tl:])
        if has_lat:
            acc = acc + _dot(vTl_ref[0, 0], p_ref[0:tl])
        return acc[0:V_HEAD_DIM] * (1.0 / acc[V_HEAD_DIM:V_HEAD_DIM + 1])

    def block(j, carry):
        qT = qT_ref[0, 0, j]
        o1 = softmax_v(jnp.where(row < HEAD_DIM, qT, zero))
        o2 = softmax_v(jnp.where(row >= HEAD_DIM, qT, zero))
        o = o1 - lam * o2
        ms = jnp.sum(o * o, axis=0, keepdims=True) * (1.0 / V_HEAD_DIM)
        on = (o * lax.rsqrt(ms + EPS)).T * out_gain
        start = pl.multiple_of(j * MXU_DIM, MXU_DIM)
        o_ref[0, pl.ds(start, MXU_DIM), :] = on.astype(BF16)
        return carry

    lax.fori_loop(0, n_blk, block, 0)


def _attention(qT, k_lat, k_ctx, vT_lat, vT_ctx, lam_rows, subln_g, lam_init):
    b, _, n_blk, _, _ = qT.shape
    tq = n_blk * MXU_DIM
    tc = k_ctx.shape[2]
    has_lat = k_lat is not None
    tl = k_lat.shape[2] if has_lat else 0
    head5 = lambda i, h: (i, h, 0, 0, 0)
    head4 = lambda i, h: (i, h, 0, 0)
    in_specs = [pl.BlockSpec((1, 1, n_blk, LANES, MXU_DIM), head5)]
    args = [qT]
    if has_lat:
        in_specs += [pl.BlockSpec((1, 1, tl, LANES), head4), pl.BlockSpec((1, 1, tc, LANES), head4),
                     pl.BlockSpec((1, 1, VT_ROWS, tl), head4), pl.BlockSpec((1, 1, VT_ROWS, tc), head4)]
        args += [k_lat, k_ctx, vT_lat, vT_ctx]
    else:
        in_specs += [pl.BlockSpec((1, 1, tc, LANES), head4), pl.BlockSpec((1, 1, VT_ROWS, tc), head4)]
        args += [k_ctx, vT_ctx]
    in_specs += [_const_spec((4, HEAD_DIM)), _const_spec((1, V_HEAD_DIM))]
    args += [lam_rows, subln_g]
    return pl.pallas_call(
        functools.partial(_attn_kernel, has_lat=has_lat, lam_init=lam_init),
        grid=(b, N_HEADS),
        in_specs=in_specs,
        out_specs=pl.BlockSpec((1, tq, V_HEAD_DIM), lambda i, h: (i, 0, h)),
        out_shape=jax.ShapeDtypeStruct((b, tq, ATTN_WIDTH), BF16),
        scratch_shapes=[pltpu.VMEM((tl + tc, MXU_DIM), F32), pltpu.VMEM((tl + tc, MXU_DIM), BF16)],
        compiler_params=_params(2),
        name="diff_attention" if has_lat else "diff_attention_ctx",
    )(*args)


def _fourier_kernel(f_ref, wt_ref, cs_ref, o_ref, xcs_ref, *, scale, tr):
    t = f_ref.shape[1]
    cs = cs_ref[...]
    for g in range(N_FOURIER_GROUPS):
        cols = slice(g * FOURIER_GROUP_DIM, (g + 1) * FOURIER_GROUP_DIM)
        r = _dot(f_ref[0, :, cols], cs)
        xcs_ref[0:t, cols] = r[:, 0:FOURIER_GROUP_DIM].astype(BF16)
        xcs_ref[t:2 * t, cols] = r[:, FOURIER_GROUP_DIM:].astype(BF16)
    for i in range(t // tr):
        rows = slice(i * tr, (i + 1) * tr)
        o_ref[0, rows, :] = (_dot(wt_ref[rows, :], xcs_ref[...]) * scale).astype(BF16)


def _fourier(f, wt, cs):
    b, t, w = f.shape
    tr = min(t, 512)
    scale = 1.0 / math.sqrt(t * FOURIER_GROUP_DIM)
    return pl.pallas_call(
        functools.partial(_fourier_kernel, scale=scale, tr=tr),
        grid=(b,),
        in_specs=[
            pl.BlockSpec((1, t, w), lambda i: (i, 0, 0)),
            _const_spec((t, 2 * t)),
            _const_spec((FOURIER_GROUP_DIM, 2 * FOURIER_GROUP_DIM)),
        ],
        out_specs=pl.BlockSpec((1, t, w), lambda i: (i, 0, 0)),
        out_shape=jax.ShapeDtypeStruct((b, t, w), BF16),
        scratch_shapes=[pltpu.VMEM((2 * t, w), BF16)],
        compiler_params=_params(1),
        name="fourier_mix",
    )(f, wt, cs)


def _merge_ffn_kernel(o_ref, mix_ref, gate_ref, x_ref, mod_ref, g2_ref, wpa_ref, wpf_ref, wout_ref,
                      wgu_ref, wdn_ref, out_ref, act_ref):
    d = D_MODEL
    mod = mod_ref[0]
    y_a = _dot(o_ref[0], wpa_ref[...])
    y_f = _dot(mix_ref[0], wpf_ref[...])
    g_a = gate_ref[0, :, 0:d].astype(F32)
    g_f = gate_ref[0, :, d:2 * d].astype(F32)
    merged = (g_a * y_a + g_f * y_f).astype(BF16)
    x1 = x_ref[0] + mod[:, 2 * d:3 * d] * _dot(merged, wout_ref[...])

    h2 = _norm_modulate(x1, g2_ref[...], mod[:, 3 * d:4 * d], mod[:, 4 * d:5 * d]).astype(BF16)
    cw = MXU_DIM
    for c in range(D_FF // cw):
        gate = _dot(h2, wgu_ref[:, c * cw:(c + 1) * cw])
        up = _dot(h2, wgu_ref[:, D_FF + c * cw:D_FF + (c + 1) * cw])
        act_ref[:, c * cw:(c + 1) * cw] = (gate * _sigmoid(gate) * up).astype(BF16)
    out_ref[0] = x1 + mod[:, 5 * d:6 * d] * _dot(act_ref[...], wdn_ref[...])


def _merge_ffn(o, mix, gates, x, mod, per_batch_mod, g2, wpa, wpf, wout, wgu, wdn, *, tm):
    b, t, d = x.shape
    mod_map = (lambda i, j: (i, 0, 0)) if per_batch_mod else (lambda i, j: (0, 0, 0))
    tok = lambda i, j: (i, j, 0)
    return pl.pallas_call(
        _merge_ffn_kernel,
        grid=(b, t // tm),
        in_specs=[
            pl.BlockSpec((1, tm, ATTN_WIDTH), tok),
            pl.BlockSpec((1, tm, FOURIER_WIDTH), tok),
            pl.BlockSpec((1, tm, 2 * d), tok),
            pl.BlockSpec((1, tm, d), tok),
            pl.BlockSpec((1, 1, N_MOD * d), mod_map),
            _const_spec((1, d)),
            _const_spec(wpa.shape), _const_spec(wpf.shape), _const_spec(wout.shape),
            _const_spec(wgu.shape), _const_spec(wdn.shape),
        ],
        out_specs=pl.BlockSpec((1, tm, d), tok),
        out_shape=jax.ShapeDtypeStruct((b, t, d), F32),
        scratch_shapes=[pltpu.VMEM((tm, D_FF), BF16)],
        compiler_params=_params(2),
        name="merge_ffn",
    )(o, mix, gates, x, mod, g2, wpa, wpf, wout, wgu, wdn)


def _rope_tables(n_tokens):
    rows = n_tokens // GRID_W
    r, col = np.meshgrid(np.arange(rows), np.arange(GRID_W), indexing="ij")
    r = r.reshape(-1).astype(np.float32)
    col = col.reshape(-1).astype(np.float32)
    axis_dim = HEAD_DIM // 2
    inv_freq = (ROPE_THETA ** (-np.arange(0, axis_dim, 2, dtype=np.float32) / axis_dim)).astype(np.float32)
    ang_r = r[:, None] * inv_freq[None, :]
    ang_c = col[:, None] * inv_freq[None, :]
    ang = np.concatenate([ang_r, ang_r, ang_c, ang_c], axis=-1)
    sign = np.where((np.arange(HEAD_DIM) % 32) < 16, -1.0, 1.0).astype(np.float32)
    cos = np.cos(ang).astype(np.float32)
    sin = (np.sin(ang) * sign).astype(np.float32)
    return np.tile(cos, (1, 2)), np.tile(sin, (1, 2))


def _dft_tables(n):
    kt = (np.arange(n)[:, None] * np.arange(n)[None, :]) % n
    ang = kt.astype(np.float64) * (2.0 * np.pi / n)
    return np.concatenate([np.cos(ang), -np.sin(ang)], axis=1).astype(np.float32)


def _channel_dft_table():
    n = FOURIER_GROUP_DIM
    kt = (np.arange(n)[:, None] * np.arange(n)[None, :]) % n
    ang = kt.astype(np.float64) * (2.0 * np.pi / n)
    return np.concatenate([np.cos(ang), np.sin(ang)], axis=1).astype(np.float32)


def _group_sum_matrix():
    g = np.arange(MXU_DIM) // HEAD_DIM
    return (g[:, None] == g[None, :]).astype(np.float32)


def kernel(x, c, ctx, c_ctx, w_ada, b_ada, norm1_g, norm2_g, w_in, q_norm_g, k_norm_g,
           lambda_q1, lambda_k1, lambda_q2, lambda_k2, subln_g, w_proj_attn, w_proj_fourier,
           w_out, w_gate_up, w_down):
    b, t, d = x.shape
    tc = ctx.shape[1]
    depth = w_ada.shape[0]
    assert b + 1 <= MOD_ROWS and d == D_MODEL

    cc = jnp.zeros((MOD_ROWS, d), F32).at[:b].set(c).at[b].set(c_ctx)
    mod = _modulation(cc, w_ada, b_ada)

    cos_np, sin_np = _rope_tables(t)
    cos_lat, sin_lat = jnp.asarray(cos_np), jnp.asarray(sin_np)
    cos_ctx = jnp.ones((tc, LANES), F32)
    sin_ctx = jnp.zeros((tc, LANES), F32)
    wt_lat = jnp.asarray(_dft_tables(t)).astype(BF16)
    wt_ctx = jnp.asarray(_dft_tables(tc)).astype(BF16)
    cs = jnp.asarray(_channel_dft_table()).astype(BF16)
    bd = jnp.asarray(_group_sum_matrix()).astype(BF16)

    cx = ctx
    for l in range(depth):
        last = l == depth - 1
        lam_init = 0.8 - 0.6 * math.exp(-0.3 * l)
        mod_lat = mod[l, :b][:, None, :]
        mod_ctx = mod[l, b:b + 1][None]
        g1 = norm1_g[l][None]
        g2 = norm2_g[l][None]
        qg = jnp.tile(q_norm_g[l], 2)[None]
        kg = jnp.tile(k_norm_g[l], 2)[None]
        w_in_l = w_in[l].astype(BF16)
        lam_rows = jnp.stack([lambda_q1[l], lambda_k1[l], lambda_q2[l], lambda_k2[l]])
        sg = subln_g[l][None]
        weights = (w_proj_attn[l].astype(BF16), w_proj_fourier[l].astype(BF16), w_out[l].astype(BF16),
                   w_gate_up[l].astype(BF16), w_down[l].astype(BF16))

        if last:
            k_ctx, vT_ctx = _inproj(cx, mod_ctx, False, g1, w_in_l[:, :KV_COLS], cos_ctx, sin_ctx,
                                    qg, kg, bd, tm=tc, kv_only=True)
        else:
            qT_ctx, k_ctx, vT_ctx, f_ctx, gates_ctx = _inproj(
                cx, mod_ctx, False, g1, w_in_l, cos_ctx, sin_ctx, qg, kg, bd, tm=tc, kv_only=False)
        qT, k_lat, vT_lat, f_lat, gates = _inproj(
            x, mod_lat, True, g1, w_in_l, cos_lat, sin_lat, qg, kg, bd, tm=512, kv_only=False)

        o = _attention(qT, k_lat, k_ctx, vT_lat, vT_ctx, lam_rows, sg, lam_init)
        mix = _fourier(f_lat, wt_lat, cs)
        x = _merge_ffn(o, mix, gates, x, mod_lat, True, g2, *weights, tm=256)

        if not last:
            o_ctx = _attention(qT_ctx, None, k_ctx, None, vT_ctx, lam_rows, sg, lam_init)
            mix_ctx = _fourier(f_ctx, wt_ctx, cs)
            cx = _merge_ffn(o_ctx, mix_ctx, gates_ctx, cx, mod_ctx, False, g2, *weights, tm=tc)
    return x
```

```python
import functools
import math

import jax
import jax.numpy as jnp
import numpy as np
from jax import lax
from jax.experimental import pallas as pl
from jax.experimental.pallas import tpu as pltpu

F32 = jnp.float32
BF16 = jnp.bfloat16

D_MODEL = 1024
N_HEADS = 8
HEAD_DIM = 64
V_HEAD_DIM = 2 * HEAD_DIM
ATTN_WIDTH = N_HEADS * V_HEAD_DIM
N_FOURIER_GROUPS = 4
FOURIER_GROUP_DIM = 128
FOURIER_WIDTH = N_FOURIER_GROUPS * FOURIER_GROUP_DIM
KV_COLS = 2 * ATTN_WIDTH
Q_COL0 = KV_COLS
F_COL0 = 3 * ATTN_WIDTH
G_COL0 = F_COL0 + FOURIER_WIDTH
IN_WIDTH = G_COL0 + 2 * D_MODEL
D_FF = 2816
N_MOD = 6
GRID_W = 64
ROPE_THETA = 10000.0
EPS = 1e-6

LANES = 128
MXU_DIM = 256
ONES_ROWS = 16
VT_ROWS = V_HEAD_DIM + ONES_ROWS
MOD_ROWS = 24
VMEM_LIMIT = 56 * 1024 * 1024
LOG2E = 1.4426950408889634
ATTN_SLOTS = 4


def _params(n_axes):
    return pltpu.CompilerParams(dimension_semantics=("arbitrary",) * n_axes,
                                vmem_limit_bytes=VMEM_LIMIT)


def _const_spec(shape):
    nd = len(shape)
    return pl.BlockSpec(shape, lambda *_: (0,) * nd, pipeline_mode=pl.Buffered(1))


def _dot(a, b):
    return jnp.dot(a, b, preferred_element_type=F32)


def _sigmoid(x):
    return 1.0 / (1.0 + jnp.exp(-x))


def _mod_kernel(c_ref, w_ref, b_ref, o_ref):
    c = c_ref[...]
    s = c * _sigmoid(c)
    s_hi = s.astype(BF16)
    s_lo = (s - s_hi.astype(F32)).astype(BF16)
    w = w_ref[0]
    w_hi = w.astype(BF16)
    w_lo = (w - w_hi.astype(F32)).astype(BF16)
    acc = _dot(s_hi, w_hi) + _dot(s_hi, w_lo) + _dot(s_lo, w_hi)
    o_ref[0] = acc + b_ref[0]


def _modulation(cc, w_ada, b_ada):
    depth, d, n = w_ada.shape
    tn = 1024
    return pl.pallas_call(
        _mod_kernel,
        grid=(depth, n // tn),
        in_specs=[
            pl.BlockSpec((MOD_ROWS, d), lambda l, j: (0, 0)),
            pl.BlockSpec((1, d, tn), lambda l, j: (l, 0, j)),
            pl.BlockSpec((1, 1, tn), lambda l, j: (l, 0, j)),
        ],
        out_specs=pl.BlockSpec((1, MOD_ROWS, tn), lambda l, j: (l, 0, j)),
        out_shape=jax.ShapeDtypeStruct((depth, MOD_ROWS, n), F32),
        compiler_params=_params(2),
        name="adaln_modulation",
    )(cc, w_ada, b_ada.reshape(depth, 1, n))


def _norm_modulate(x, g, shift, scale):
    ms = jnp.mean(x * x, axis=-1, keepdims=True)
    return (x * lax.rsqrt(ms + EPS) * g) * (1.0 + scale) + shift


def _inproj_kernel(x_ref, mod_ref, g1_ref, w_ref, cos_ref, sin_ref, qg_ref, kg_ref, bd_ref,
                   *out_refs, kv_only):
    if kv_only:
        k_ref, vT_ref = out_refs
    else:
        qT_ref, k_ref, vT_ref, f_ref, gate_ref = out_refs
    tm = x_ref.shape[1]
    mod = mod_ref[0]
    h = _norm_modulate(x_ref[0], g1_ref[...], mod[:, 0:D_MODEL], mod[:, D_MODEL:2 * D_MODEL])
    hb = h.astype(BF16)
    cos = cos_ref[...]
    sin = sin_ref[...]
    lane = lax.broadcasted_iota(jnp.int32, (tm, LANES), 1)
    first_half = jnp.bitwise_and(lane, 31) < 16
    bd = bd_ref[...]

    def normed_rope(col0, gain):
        p = _dot(hb, w_ref[:, col0:col0 + MXU_DIM])
        ssq = _dot((p * p).astype(BF16), bd)
        y = p * lax.rsqrt(ssq * (1.0 / HEAD_DIM) + EPS)
        slabs = []
        for s in range(MXU_DIM // LANES):
            ys = y[:, s * LANES:(s + 1) * LANES] * gain
            rot = jnp.where(first_half, pltpu.roll(ys, LANES - 16, 1), pltpu.roll(ys, 16, 1))
            slabs.append(ys * cos + rot * sin)
        return slabs

    kg = kg_ref[...]
    for c in range(ATTN_WIDTH // MXU_DIM):
        slabs = normed_rope(c * MXU_DIM, kg)
        for s, slab in enumerate(slabs):
            k_ref[0, 2 * c + s] = slab.astype(BF16)

    ones = jnp.ones((ONES_ROWS, tm), BF16)
    for c in range(ATTN_WIDTH // MXU_DIM):
        v = _dot(hb, w_ref[:, ATTN_WIDTH + c * MXU_DIM:ATTN_WIDTH + (c + 1) * MXU_DIM])
        for s in range(MXU_DIM // LANES):
            hd = 2 * c + s
            vT_ref[0, hd, 0:V_HEAD_DIM, :] = v[:, s * LANES:(s + 1) * LANES].T.astype(BF16)
            vT_ref[0, hd, V_HEAD_DIM:VT_ROWS, :] = ones

    if kv_only:
        return

    qg = qg_ref[...] * (HEAD_DIM ** -0.5 * LOG2E)
    for c in range(ATTN_WIDTH // MXU_DIM):
        slabs = normed_rope(Q_COL0 + c * MXU_DIM, qg)
        for s, slab in enumerate(slabs):
            st = slab.T.astype(BF16)
            for j in range(tm // MXU_DIM):
                qT_ref[0, 2 * c + s, j] = st[:, j * MXU_DIM:(j + 1) * MXU_DIM]

    f_ref[0] = _dot(hb, w_ref[:, F_COL0:F_COL0 + FOURIER_WIDTH]).astype(BF16)

    gw = 512
    for c in range(2 * D_MODEL // gw):
        g = _dot(hb, w_ref[:, G_COL0 + c * gw:G_COL0 + (c + 1) * gw])
        gate_ref[0, :, c * gw:(c + 1) * gw] = _sigmoid(g).astype(BF16)


def _inproj(x, mod, per_batch_mod, g1, w, cos, sin, qg, kg, bd, *, tm, kv_only):
    b, t, d = x.shape
    nq = tm // MXU_DIM
    n_w = w.shape[1]
    mod_map = (lambda i, j: (i, 0, 0)) if per_batch_mod else (lambda i, j: (0, 0, 0))
    in_specs = [
        pl.BlockSpec((1, tm, d), lambda i, j: (i, j, 0)),
        pl.BlockSpec((1, 1, N_MOD * d), mod_map),
        _const_spec((1, d)),
        _const_spec((d, n_w)),
        pl.BlockSpec((tm, LANES), lambda i, j: (j, 0)),
        pl.BlockSpec((tm, LANES), lambda i, j: (j, 0)),
        _const_spec((1, LANES)),
        _const_spec((1, LANES)),
        _const_spec((MXU_DIM, MXU_DIM)),
    ]
    k_spec = pl.BlockSpec((1, N_HEADS, tm, LANES), lambda i, j: (i, 0, j, 0))
    vT_spec = pl.BlockSpec((1, N_HEADS, VT_ROWS, tm), lambda i, j: (i, 0, 0, j))
    k_shape = jax.ShapeDtypeStruct((b, N_HEADS, t, LANES), BF16)
    vT_shape = jax.ShapeDtypeStruct((b, N_HEADS, VT_ROWS, t), BF16)
    if kv_only:
        out_specs = [k_spec, vT_spec]
        out_shape = [k_shape, vT_shape]
    else:
        out_specs = [
            pl.BlockSpec((1, N_HEADS, nq, LANES, MXU_DIM), lambda i, j: (i, 0, j, 0, 0)),
            k_spec, vT_spec,
            pl.BlockSpec((1, tm, FOURIER_WIDTH), lambda i, j: (i, j, 0)),
            pl.BlockSpec((1, tm, 2 * d), lambda i, j: (i, j, 0)),
        ]
        out_shape = [
            jax.ShapeDtypeStruct((b, N_HEADS, t // MXU_DIM, LANES, MXU_DIM), BF16),
            k_shape, vT_shape,
            jax.ShapeDtypeStruct((b, t, FOURIER_WIDTH), BF16),
            jax.ShapeDtypeStruct((b, t, 2 * d), BF16),
        ]
    return pl.pallas_call(
        functools.partial(_inproj_kernel, kv_only=kv_only),
        grid=(b, t // tm),
        in_specs=in_specs,
        out_specs=out_specs,
        out_shape=out_shape,
        compiler_params=_params(2),
        name="inproj_kv" if kv_only else "inproj",
    )(x, mod, g1, w, cos, sin, qg, kg, bd)


def _attn_kernel(*refs, has_lat, lam_init):
    if has_lat:
        qT_ref, kl_ref, kc_ref, vTl_ref, vTc_ref, lam_ref, sg_ref, o_ref = refs[:8]
        tl = kl_ref.shape[2]
    else:
        qT_ref, kc_ref, vTc_ref, lam_ref, sg_ref, o_ref = refs[:6]
        tl = 0
    m_ref = refs[-2 * ATTN_SLOTS - 1]
    s_refs = refs[-2 * ATTN_SLOTS:-ATTN_SLOTS]
    p_refs = refs[-ATTN_SLOTS:]
    tc = kc_ref.shape[2]
    n_blk = qT_ref.shape[2]
    ch = MXU_DIM
    lp = lam_ref[...]
    lam = (jnp.exp(jnp.sum(lp[0:1] * lp[1:2], axis=-1, keepdims=True))
           - jnp.exp(jnp.sum(lp[2:3] * lp[3:4], axis=-1, keepdims=True)) + lam_init)
    out_gain = sg_ref[...] * (1.0 - lam_init)
    row = lax.broadcasted_iota(jnp.int32, (V_HEAD_DIM, MXU_DIM), 0)
    zero = jnp.zeros((V_HEAD_DIM, MXU_DIM), BF16)

    chunks = ([(c * ch, kl_ref, vTl_ref, c * ch) for c in range(tl // ch)]
              + [(tl + c * ch, kc_ref, vTc_ref, c * ch) for c in range(tc // ch)])

    def score_chunk(slot, w, chunk):
        dst, k_ref, _, src = chunk
        sc = jnp.dot(k_ref[0, 0, src:src + ch, :], w, preferred_element_type=F32)
        s_refs[slot][dst:dst + ch, :] = sc
        return jnp.max(sc, axis=0, keepdims=True)

    def prob_chunk(slot, m, chunk):
        dst = chunk[0]
        p_refs[slot][dst:dst + ch, :] = jnp.exp2(s_refs[slot][dst:dst + ch, :] - m).astype(BF16)

    def value_chunk(slot, chunk):
        dst, _, vT_ref, src = chunk
        return jnp.dot(vT_ref[0, 0, :, src:src + ch], p_refs[slot][dst:dst + ch, :],
                       preferred_element_type=F32)

    def rows_of(j):
        if isinstance(j, int):
            return slice(j * MXU_DIM, (j + 1) * MXU_DIM)
        return pl.ds(pl.multiple_of(j * MXU_DIM, MXU_DIM), MXU_DIM)

    def finish_block(j, acc1, acc2):
        o1 = acc1[0:V_HEAD_DIM] * (1.0 / acc1[V_HEAD_DIM:V_HEAD_DIM + 1])
        o2 = acc2[0:V_HEAD_DIM] * (1.0 / acc2[V_HEAD_DIM:V_HEAD_DIM + 1])
        o = o1 - lam * o2
        ms = jnp.sum(o * o, axis=0, keepdims=True) * (1.0 / V_HEAD_DIM)
        on = (o * lax.rsqrt(ms + EPS)).T * out_gain
        o_ref[0, rows_of(j), :] = on.astype(BF16)

    def stage(j, par, do_values, do_scores, do_probs):
        other = 1 - par
        if do_scores:
            qT = qT_ref[0, 0, j + 1]
            w1 = jnp.where(row < HEAD_DIM, qT, zero)
            w2 = jnp.where(row >= HEAD_DIM, qT, zero)
        if do_probs:
            pm1 = m_ref[2 * par, 0:1, :]
            pm2 = m_ref[2 * par + 1, 0:1, :]
        acc1 = acc2 = m1 = m2 = None
        for chunk in chunks:
            if do_values:
                a1 = value_chunk(2 * other, chunk)
                a2 = value_chunk(2 * other + 1, chunk)
                acc1 = a1 if acc1 is None else acc1 + a1
                acc2 = a2 if acc2 is None else acc2 + a2
            if do_scores:
                c1 = score_chunk(2 * other, w1, chunk)
                c2 = score_chunk(2 * other + 1, w2, chunk)
                m1 = c1 if m1 is None else jnp.maximum(m1, c1)
                m2 = c2 if m2 is None else jnp.maximum(m2, c2)
            if do_probs:
                prob_chunk(2 * par, pm1, chunk)
                prob_chunk(2 * par + 1, pm2, chunk)
        if do_scores:
            m_ref[2 * other] = jnp.broadcast_to(m1, (8, MXU_DIM))
            m_ref[2 * other + 1] = jnp.broadcast_to(m2, (8, MXU_DIM))
        if do_values:
            finish_block(j - 1, acc1, acc2)

    def static_stage(j):
        stage(j, j % 2, do_values=j >= 1, do_scores=j + 1 < n_blk, do_probs=0 <= j < n_blk)

    static_stage(-1)
    static_stage(0)
    if n_blk >= 3:
        def body(j, carry):
            @pl.when(j % 2 == 0)
            def _():
                stage(j, 0, True, True, True)

            @pl.when(j % 2 == 1)
            def _():
                stage(j, 1, True, True, True)
            return carry
        lax.fori_loop(1, n_blk - 1, body, 0)
    for j in range(max(1, n_blk - 1), n_blk + 1):
        static_stage(j)


def _attention(qT, k_lat, k_ctx, vT_lat, vT_ctx, lam_rows, subln_g, lam_init):
    b, _, n_blk, _, _ = qT.shape
    tq = n_blk * MXU_DIM
    tc = k_ctx.shape[2]
    has_lat = k_lat is not None
    tl = k_lat.shape[2] if has_lat else 0
    head5 = lambda i, h: (i, h, 0, 0, 0)
    head4 = lambda i, h: (i, h, 0, 0)
    in_specs = [pl.BlockSpec((1, 1, n_blk, LANES, MXU_DIM), head5)]
    args = [qT]
    if has_lat:
        in_specs += [pl.BlockSpec((1, 1, tl, LANES), head4), pl.BlockSpec((1, 1, tc, LANES), head4),
                     pl.BlockSpec((1, 1, VT_ROWS, tl), head4), pl.BlockSpec((1, 1, VT_ROWS, tc), head4)]
        args += [k_lat, k_ctx, vT_lat, vT_ctx]
    else:
        in_specs += [pl.BlockSpec((1, 1, tc, LANES), head4), pl.BlockSpec((1, 1, VT_ROWS, tc), head4)]
        args += [k_ctx, vT_ctx]
    in_specs += [_const_spec((4, HEAD_DIM)), _const_spec((1, V_HEAD_DIM))]
    args += [lam_rows, subln_g]
    return pl.pallas_call(
        functools.partial(_attn_kernel, has_lat=has_lat, lam_init=lam_init),
        grid=(b, N_HEADS),
        in_specs=in_specs,
        out_specs=pl.BlockSpec((1, tq, V_HEAD_DIM), lambda i, h: (i, 0, h)),
        out_shape=jax.ShapeDtypeStruct((b, tq, ATTN_WIDTH), BF16),
        scratch_shapes=([pltpu.VMEM((ATTN_SLOTS, 8, MXU_DIM), F32)]
                        + [pltpu.VMEM((tl + tc, MXU_DIM), F32)] * ATTN_SLOTS
                        + [pltpu.VMEM((tl + tc, MXU_DIM), BF16)] * ATTN_SLOTS),
        compiler_params=_params(2),
        name="diff_attention" if has_lat else "diff_attention_ctx",
    )(*args)


def _fourier_kernel(f_ref, wt_ref, cs_ref, o_ref, xcs_ref, *, scale, tr):
    t = f_ref.shape[1]
    cs = cs_ref[...]
    for g in range(N_FOURIER_GROUPS):
        cols = slice(g * FOURIER_GROUP_DIM, (g + 1) * FOURIER_GROUP_DIM)
        r = _dot(f_ref[0, :, cols], cs)
        xcs_ref[0:t, cols] = r[:, 0:FOURIER_GROUP_DIM].astype(BF16)
        xcs_ref[t:2 * t, cols] = r[:, FOURIER_GROUP_DIM:].astype(BF16)
    for i in range(t // tr):
        rows = slice(i * tr, (i + 1) * tr)
        o_ref[0, rows, :] = (_dot(wt_ref[rows, :], xcs_ref[...]) * scale).astype(BF16)


def _fourier(f, wt, cs):
    b, t, w = f.shape
    tr = min(t, 512)
    scale = 1.0 / math.sqrt(t * FOURIER_GROUP_DIM)
    return pl.pallas_call(
        functools.partial(_fourier_kernel, scale=scale, tr=tr),
        grid=(b,),
        in_specs=[
            pl.BlockSpec((1, t, w), lambda i: (i, 0, 0)),
            _const_spec((t, 2 * t)),
            _const_spec((FOURIER_GROUP_DIM, 2 * FOURIER_GROUP_DIM)),
        ],
        out_specs=pl.BlockSpec((1, t, w), lambda i: (i, 0, 0)),
        out_shape=jax.ShapeDtypeStruct((b, t, w), BF16),
        scratch_shapes=[pltpu.VMEM((2 * t, w), BF16)],
        compiler_params=_params(1),
        name="fourier_mix",
    )(f, wt, cs)


def _merge_ffn_kernel(o_ref, mix_ref, gate_ref, x_ref, mod_ref, g2_ref, wpa_ref, wpf_ref, wout_ref,
                      wgu_ref, wdn_ref, out_ref, act_ref):
    d = D_MODEL
    mod = mod_ref[0]
    y_a = _dot(o_ref[0], wpa_ref[...])
    y_f = _dot(mix_ref[0], wpf_ref[...])
    g_a = gate_ref[0, :, 0:d].astype(F32)
    g_f = gate_ref[0, :, d:2 * d].astype(F32)
    merged = (g_a * y_a + g_f * y_f).astype(BF16)
    x1 = x_ref[0] + mod[:, 2 * d:3 * d] * _dot(merged, wout_ref[...])

    h2 = _norm_modulate(x1, g2_ref[...], mod[:, 3 * d:4 * d], mod[:, 4 * d:5 * d]).astype(BF16)
    cw = MXU_DIM
    for c in range(D_FF // cw):
        gate = _dot(h2, wgu_ref[:, c * cw:(c + 1) * cw])
        up = _dot(h2, wgu_ref[:, D_FF + c * cw:D_FF + (c + 1) * cw])
        act_ref[:, c * cw:(c + 1) * cw] = (gate * _sigmoid(gate) * up).astype(BF16)
    out_ref[0] = x1 + mod[:, 5 * d:6 * d] * _dot(act_ref[...], wdn_ref[...])


def _merge_ffn(o, mix, gates, x, mod, per_batch_mod, g2, wpa, wpf, wout, wgu, wdn, *, tm):
    b, t, d = x.shape
    mod_map = (lambda i, j: (i, 0, 0)) if per_batch_mod else (lambda i, j: (0, 0, 0))
    tok = lambda i, j: (i, j, 0)
    return pl.pallas_call(
        _merge_ffn_kernel,
        grid=(b, t // tm),
        in_specs=[
            pl.BlockSpec((1, tm, ATTN_WIDTH), tok),
            pl.BlockSpec((1, tm, FOURIER_WIDTH), tok),
            pl.BlockSpec((1, tm, 2 * d), tok),
            pl.BlockSpec((1, tm, d), tok),
            pl.BlockSpec((1, 1, N_MOD * d), mod_map),
            _const_spec((1, d)),
            _const_spec(wpa.shape), _const_spec(wpf.shape), _const_spec(wout.shape),
            _const_spec(wgu.shape), _const_spec(wdn.shape),
        ],
        out_specs=pl.BlockSpec((1, tm, d), tok),
        out_shape=jax.ShapeDtypeStruct((b, t, d), F32),
        scratch_shapes=[pltpu.VMEM((tm, D_FF), BF16)],
        compiler_params=_params(2),
        name="merge_ffn",
    )(o, mix, gates, x, mod, g2, wpa, wpf, wout, wgu, wdn)


def _rope_tables(n_tokens):
    rows = n_tokens // GRID_W
    r, col = np.meshgrid(np.arange(rows), np.arange(GRID_W), indexing="ij")
    r = r.reshape(-1).astype(np.float32)
    col = col.reshape(-1).astype(np.float32)
    axis_dim = HEAD_DIM // 2
    inv_freq = (ROPE_THETA ** (-np.arange(0, axis_dim, 2, dtype=np.float32) / axis_dim)).astype(np.float32)
    ang_r = r[:, None] * inv_freq[None, :]
    ang_c = col[:, None] * inv_freq[None, :]
    ang = np.concatenate([ang_r, ang_r, ang_c, ang_c], axis=-1)
    sign = np.where((np.arange(HEAD_DIM) % 32) < 16, -1.0, 1.0).astype(np.float32)
    cos = np.cos(ang).astype(np.float32)
    sin = (np.sin(ang) * sign).astype(np.float32)
    return np.tile(cos, (1, 2)), np.tile(sin, (1, 2))


def _dft_tables(n):
    kt = (np.arange(n)[:, None] * np.arange(n)[None, :]) % n
    ang = kt.astype(np.float64) * (2.0 * np.pi / n)
    return np.concatenate([np.cos(ang), -np.sin(ang)], axis=1).astype(np.float32)


def _channel_dft_table():
    n = FOURIER_GROUP_DIM
    kt = (np.arange(n)[:, None] * np.arange(n)[None, :]) % n
    ang = kt.astype(np.float64) * (2.0 * np.pi / n)
    return np.concatenate([np.cos(ang), np.sin(ang)], axis=1).astype(np.float32)


def _group_sum_matrix():
    g = np.arange(MXU_DIM) // HEAD_DIM
    return (g[:, None] == g[None, :]).astype(np.float32)


def kernel(x, c, ctx, c_ctx, w_ada, b_ada, norm1_g, norm2_g, w_in, q_norm_g, k_norm_g,
           lambda_q1, lambda_k1, lambda_q2, lambda_k2, subln_g, w_proj_attn, w_proj_fourier,
           w_out, w_gate_up, w_down):
    b, t, d = x.shape
    tc = ctx.shape[1]
    depth = w_ada.shape[0]
    assert b + 1 <= MOD_ROWS and d == D_MODEL

    cc = jnp.zeros((MOD_ROWS, d), F32).at[:b].set(c).at[b].set(c_ctx)
    mod = _modulation(cc, w_ada, b_ada)

    cos_np, sin_np = _rope_tables(t)
    cos_lat, sin_lat = jnp.asarray(cos_np), jnp.asarray(sin_np)
    cos_ctx = jnp.ones((tc, LANES), F32)
    sin_ctx = jnp.zeros((tc, LANES), F32)
    wt_lat = jnp.asarray(_dft_tables(t)).astype(BF16)
    wt_ctx = jnp.asarray(_dft_tables(tc)).astype(BF16)
    cs = jnp.asarray(_channel_dft_table()).astype(BF16)
    bd = jnp.asarray(_group_sum_matrix()).astype(BF16)

    cx = ctx
    for l in range(depth):
        last = l == depth - 1
        lam_init = 0.8 - 0.6 * math.exp(-0.3 * l)
        mod_lat = mod[l, :b][:, None, :]
        mod_ctx = mod[l, b:b + 1][None]
        g1 = norm1_g[l][None]
        g2 = norm2_g[l][None]
        qg = jnp.tile(q_norm_g[l], 2)[None]
        kg = jnp.tile(k_norm_g[l], 2)[None]
        w_in_l = w_in[l].astype(BF16)
        lam_rows = jnp.stack([lambda_q1[l], lambda_k1[l], lambda_q2[l], lambda_k2[l]])
        sg = subln_g[l][None]
        weights = (w_proj_attn[l].astype(BF16), w_proj_fourier[l].astype(BF16), w_out[l].astype(BF16),
                   w_gate_up[l].astype(BF16), w_down[l].astype(BF16))

        if last:
            k_ctx, vT_ctx = _inproj(cx, mod_ctx, False, g1, w_in_l[:, :KV_COLS], cos_ctx, sin_ctx,
                                    qg, kg, bd, tm=tc, kv_only=True)
        else:
            qT_ctx, k_ctx, vT_ctx, f_ctx, gates_ctx = _inproj(
                cx, mod_ctx, False, g1, w_in_l, cos_ctx, sin_ctx, qg, kg, bd, tm=tc, kv_only=False)
        qT, k_lat, vT_lat, f_lat, gates = _inproj(
            x, mod_lat, True, g1, w_in_l, cos_lat, sin_lat, qg, kg, bd, tm=512, kv_only=False)

        o = _attention(qT, k_lat, k_ctx, vT_lat, vT_ctx, lam_rows, sg, lam_init)
        mix = _fourier(f_lat, wt_lat, cs)
        x = _merge_ffn(o, mix, gates, x, mod_lat, True, g2, *weights, tm=256)

        if not last:
            o_ctx = _attention(qT_ctx, None, k_ctx, None, vT_ctx, lam_rows, sg, lam_init)
            mix_ctx = _fourier(f_ctx, wt_ctx, cs)
            cx = _merge_ffn(o_ctx, mix_ctx, gates_ctx, cx, mod_ctx, False, g2, *weights, tm=tc)
    return x
```

```python
import functools
import math

import jax
import jax.numpy as jnp
import numpy as np
from jax import lax
from jax.experimental import pallas as pl
from jax.experimental.pallas import tpu as pltpu

F32 = jnp.float32
BF16 = jnp.bfloat16

D_MODEL = 1024
N_HEADS = 8
HEAD_DIM = 64
V_HEAD_DIM = 2 * HEAD_DIM
ATTN_WIDTH = N_HEADS * V_HEAD_DIM
N_FOURIER_GROUPS = 4
FOURIER_GROUP_DIM = 128
FOURIER_WIDTH = N_FOURIER_GROUPS * FOURIER_GROUP_DIM
KV_COLS = 2 * ATTN_WIDTH
Q_COL0 = KV_COLS
F_COL0 = 3 * ATTN_WIDTH
G_COL0 = F_COL0 + FOURIER_WIDTH
IN_WIDTH = G_COL0 + 2 * D_MODEL
D_FF = 2816
N_MOD = 6
GRID_W = 64
ROPE_THETA = 10000.0
EPS = 1e-6

LANES = 128
MXU_DIM = 256
ONES_ROWS = 16
VT_ROWS = V_HEAD_DIM + ONES_ROWS
MOD_ROWS = 24
VMEM_LIMIT = 56 * 1024 * 1024
LOG2E = 1.4426950408889634
ATTN_SLOTS = 4
ATTN_HEADS_PER_STEP = 4


def _params(n_axes):
    return pltpu.CompilerParams(dimension_semantics=("arbitrary",) * n_axes,
                                vmem_limit_bytes=VMEM_LIMIT)


def _const_spec(shape):
    nd = len(shape)
    return pl.BlockSpec(shape, lambda *_: (0,) * nd, pipeline_mode=pl.Buffered(1))


def _dot(a, b):
    return jnp.dot(a, b, preferred_element_type=F32)


def _sigmoid(x):
    return 1.0 / (1.0 + jnp.exp(-x))


def _mod_kernel(c_ref, w_ref, b_ref, o_ref):
    c = c_ref[...]
    s = c * _sigmoid(c)
    s_hi = s.astype(BF16)
    s_lo = (s - s_hi.astype(F32)).astype(BF16)
    w = w_ref[0]
    w_hi = w.astype(BF16)
    w_lo = (w - w_hi.astype(F32)).astype(BF16)
    acc = _dot(s_hi, w_hi) + _dot(s_hi, w_lo) + _dot(s_lo, w_hi)
    o_ref[0] = acc + b_ref[0]


def _modulation(cc, w_ada, b_ada):
    depth, d, n = w_ada.shape
    tn = 1024
    return pl.pallas_call(
        _mod_kernel,
        grid=(depth, n // tn),
        in_specs=[
            pl.BlockSpec((MOD_ROWS, d), lambda l, j: (0, 0)),
            pl.BlockSpec((1, d, tn), lambda l, j: (l, 0, j)),
            pl.BlockSpec((1, 1, tn), lambda l, j: (l, 0, j)),
        ],
        out_specs=pl.BlockSpec((1, MOD_ROWS, tn), lambda l, j: (l, 0, j)),
        out_shape=jax.ShapeDtypeStruct((depth, MOD_ROWS, n), F32),
        compiler_params=_params(2),
        name="adaln_modulation",
    )(cc, w_ada, b_ada.reshape(depth, 1, n))


def _norm_modulate(x, g, shift, scale):
    ms = jnp.mean(x * x, axis=-1, keepdims=True)
    return (x * lax.rsqrt(ms + EPS) * g) * (1.0 + scale) + shift


def _inproj_kernel(x_ref, mod_ref, g1_ref, w_ref, cos_ref, sin_ref, qg_ref, kg_ref, bd_ref,
                   *out_refs, kv_only):
    if kv_only:
        k_ref, vT_ref = out_refs
    else:
        qT_ref, k_ref, vT_ref, f_ref, gate_ref = out_refs
    tm = x_ref.shape[1]
    mod = mod_ref[0]
    h = _norm_modulate(x_ref[0], g1_ref[...], mod[:, 0:D_MODEL], mod[:, D_MODEL:2 * D_MODEL])
    hb = h.astype(BF16)
    cos = cos_ref[...]
    sin = sin_ref[...]
    lane = lax.broadcasted_iota(jnp.int32, (tm, LANES), 1)
    first_half = jnp.bitwise_and(lane, 31) < 16
    bd = bd_ref[...]

    def normed_rope(p, gain):
        ssq = _dot((p * p).astype(BF16), bd)
        y = p * lax.rsqrt(ssq * (1.0 / HEAD_DIM) + EPS)
        slabs = []
        for s in range(MXU_DIM // LANES):
            ys = y[:, s * LANES:(s + 1) * LANES] * gain
            rot = jnp.where(first_half, pltpu.roll(ys, LANES - 16, 1), pltpu.roll(ys, 16, 1))
            slabs.append(ys * cos + rot * sin)
        return slabs

    kg = kg_ref[...]
    ones = jnp.ones((ONES_ROWS, tm), BF16)

    def finish_k(p, c):
        for s, slab in enumerate(normed_rope(p, kg)):
            k_ref[0, 2 * c + s] = slab.astype(BF16)

    def finish_v(v, c):
        for s in range(MXU_DIM // LANES):
            hd = 2 * c + s
            vT_ref[0, hd, 0:V_HEAD_DIM, :] = v[:, s * LANES:(s + 1) * LANES].T.astype(BF16)
            vT_ref[0, hd, V_HEAD_DIM:VT_ROWS, :] = ones

    def finish_q(p, c):
        qg = qg_ref[...] * (HEAD_DIM ** -0.5 * LOG2E)
        for s, slab in enumerate(normed_rope(p, qg)):
            st = slab.T.astype(BF16)
            for j in range(tm // MXU_DIM):
                qT_ref[0, 2 * c + s, j] = st[:, j * MXU_DIM:(j + 1) * MXU_DIM]

    def finish_f(f, c):
        f_ref[0, :, c * MXU_DIM:(c + 1) * MXU_DIM] = f.astype(BF16)

    def finish_gate(g, c):
        gate_ref[0, :, c * MXU_DIM:(c + 1) * MXU_DIM] = _sigmoid(g).astype(BF16)

    sections = [(0, ATTN_WIDTH, finish_k), (ATTN_WIDTH, ATTN_WIDTH, finish_v)]
    if not kv_only:
        sections += [(Q_COL0, ATTN_WIDTH, finish_q), (F_COL0, FOURIER_WIDTH, finish_f),
                     (G_COL0, 2 * D_MODEL, finish_gate)]
    pending = None
    for col0, width, finish in sections:
        for c in range(width // MXU_DIM):
            p = _dot(hb, w_ref[:, col0 + c * MXU_DIM:col0 + (c + 1) * MXU_DIM])
            if pending is not None:
                pending[0](pending[1], pending[2])
            pending = (finish, p, c)
    pending[0](pending[1], pending[2])


def _inproj(x, mod, per_batch_mod, g1, w, cos, sin, qg, kg, bd, *, tm, kv_only):
    b, t, d = x.shape
    nq = tm // MXU_DIM
    n_w = w.shape[1]
    mod_map = (lambda i, j: (i, 0, 0)) if per_batch_mod else (lambda i, j: (0, 0, 0))
    in_specs = [
        pl.BlockSpec((1, tm, d), lambda i, j: (i, j, 0)),
        pl.BlockSpec((1, 1, N_MOD * d), mod_map),
        _const_spec((1, d)),
        _const_spec((d, n_w)),
        pl.BlockSpec((tm, LANES), lambda i, j: (j, 0)),
        pl.BlockSpec((tm, LANES), lambda i, j: (j, 0)),
        _const_spec((1, LANES)),
        _const_spec((1, LANES)),
        _const_spec((MXU_DIM, MXU_DIM)),
    ]
    k_spec = pl.BlockSpec((1, N_HEADS, tm, LANES), lambda i, j: (i, 0, j, 0))
    vT_spec = pl.BlockSpec((1, N_HEADS, VT_ROWS, tm), lambda i, j: (i, 0, 0, j))
    k_shape = jax.ShapeDtypeStruct((b, N_HEADS, t, LANES), BF16)
    vT_shape = jax.ShapeDtypeStruct((b, N_HEADS, VT_ROWS, t), BF16)
    if kv_only:
        out_specs = [k_spec, vT_spec]
        out_shape = [k_shape, vT_shape]
    else:
        out_specs = [
            pl.BlockSpec((1, N_HEADS, nq, LANES, MXU_DIM), lambda i, j: (i, 0, j, 0, 0)),
            k_spec, vT_spec,
            pl.BlockSpec((1, tm, FOURIER_WIDTH), lambda i, j: (i, j, 0)),
            pl.BlockSpec((1, tm, 2 * d), lambda i, j: (i, j, 0)),
        ]
        out_shape = [
            jax.ShapeDtypeStruct((b, N_HEADS, t // MXU_DIM, LANES, MXU_DIM), BF16),
            k_shape, vT_shape,
            jax.ShapeDtypeStruct((b, t, FOURIER_WIDTH), BF16),
            jax.ShapeDtypeStruct((b, t, 2 * d), BF16),
        ]
    return pl.pallas_call(
        functools.partial(_inproj_kernel, kv_only=kv_only),
        grid=(b, t // tm),
        in_specs=in_specs,
        out_specs=out_specs,
        out_shape=out_shape,
        compiler_params=_params(2),
        name="inproj_kv" if kv_only else "inproj",
    )(x, mod, g1, w, cos, sin, qg, kg, bd)


def _attn_kernel(*refs, has_lat, lam_init):
    if has_lat:
        qT_ref, kl_ref, kc_ref, vTl_ref, vTc_ref, lam_ref, sg_ref, o_ref = refs[:8]
        tl = kl_ref.shape[2]
    else:
        qT_ref, kc_ref, vTc_ref, lam_ref, sg_ref, o_ref = refs[:6]
        tl = 0
    m_ref, acc_ref = refs[-2 * ATTN_SLOTS - 2], refs[-2 * ATTN_SLOTS - 1]
    s_refs = refs[-2 * ATTN_SLOTS:-ATTN_SLOTS]
    p_refs = refs[-ATTN_SLOTS:]
    tc = kc_ref.shape[2]
    n_heads, nb = qT_ref.shape[1], qT_ref.shape[2]
    n_blk = n_heads * nb
    ch = MXU_DIM

    def split(j):
        if isinstance(j, int):
            return j // nb, j % nb
        assert nb & (nb - 1) == 0
        return lax.shift_right_logical(j, nb.bit_length() - 1), lax.bitwise_and(j, nb - 1)
    lp = lam_ref[...]
    lam = (jnp.exp(jnp.sum(lp[0:1] * lp[1:2], axis=-1, keepdims=True))
           - jnp.exp(jnp.sum(lp[2:3] * lp[3:4], axis=-1, keepdims=True)) + lam_init)
    out_gain = sg_ref[...] * (1.0 - lam_init)
    row = lax.broadcasted_iota(jnp.int32, (V_HEAD_DIM, MXU_DIM), 0)
    zero = jnp.zeros((V_HEAD_DIM, MXU_DIM), BF16)

    chunks = ([(c * ch, kl_ref, vTl_ref, c * ch) for c in range(tl // ch)]
              + [(tl + c * ch, kc_ref, vTc_ref, c * ch) for c in range(tc // ch)])

    def score_chunk(slot, w, chunk, head):
        dst, k_ref, _, src = chunk
        sc = jnp.dot(k_ref[0, head, src:src + ch, :], w, preferred_element_type=F32)
        s_refs[slot][dst:dst + ch, :] = sc
        return jnp.max(sc, axis=0, keepdims=True)

    def prob_chunk(slot, m, chunk):
        dst = chunk[0]
        p_refs[slot][dst:dst + ch, :] = jnp.exp2(s_refs[slot][dst:dst + ch, :] - m).astype(BF16)

    def value_chunk(slot, chunk, head):
        dst, _, vT_ref, src = chunk
        return jnp.dot(vT_ref[0, head, :, src:src + ch], p_refs[slot][dst:dst + ch, :],
                       preferred_element_type=F32)

    def finish_block(j, acc1, acc2):
        head, blk = split(j)
        o1 = acc1[0:V_HEAD_DIM] * (1.0 / acc1[V_HEAD_DIM:V_HEAD_DIM + 1])
        o2 = acc2[0:V_HEAD_DIM] * (1.0 / acc2[V_HEAD_DIM:V_HEAD_DIM + 1])
        o = o1 - lam * o2
        ms = jnp.sum(o * o, axis=0, keepdims=True) * (1.0 / V_HEAD_DIM)
        on = (o * lax.rsqrt(ms + EPS)).T * out_gain
        if isinstance(blk, int):
            rows = slice(blk * MXU_DIM, (blk + 1) * MXU_DIM)
        else:
            rows = pl.ds(pl.multiple_of(blk * MXU_DIM, MXU_DIM), MXU_DIM)
        o_ref[0, head, rows, :] = on.astype(BF16)

    def stage(j, par, do_finish, do_values, do_scores, do_probs):
        other = 1 - par
        if do_finish:
            finish_block(j - 2, acc_ref[2 * par], acc_ref[2 * par + 1])
        if do_values:
            v_head = split(j - 1)[0]
        if do_scores:
            s_head, s_blk = split(j + 1)
            qT = qT_ref[0, s_head, s_blk]
            w1 = jnp.where(row < HEAD_DIM, qT, zero)
            w2 = jnp.where(row >= HEAD_DIM, qT, zero)
        if do_probs:
            pm1 = m_ref[2 * par, 0:1, :]
            pm2 = m_ref[2 * par + 1, 0:1, :]
        acc1 = acc2 = m1 = m2 = None
        for chunk in chunks:
            if do_values:
                a1 = value_chunk(2 * other, chunk, v_head)
                a2 = value_chunk(2 * other + 1, chunk, v_head)
                acc1 = a1 if acc1 is None else acc1 + a1
                acc2 = a2 if acc2 is None else acc2 + a2
            if do_scores:
                c1 = score_chunk(2 * other, w1, chunk, s_head)
                c2 = score_chunk(2 * other + 1, w2, chunk, s_head)
                m1 = c1 if m1 is None else jnp.maximum(m1, c1)
                m2 = c2 if m2 is None else jnp.maximum(m2, c2)
            if do_probs:
                prob_chunk(2 * par, pm1, chunk)
                prob_chunk(2 * par + 1, pm2, chunk)
        if do_scores:
            m_ref[2 * other] = jnp.broadcast_to(m1, (8, MXU_DIM))
            m_ref[2 * other + 1] = jnp.broadcast_to(m2, (8, MXU_DIM))
        if do_values:
            acc_ref[2 * other] = acc1
            acc_ref[2 * other + 1] = acc2

    def static_stage(j):
        stage(j, j % 2, do_finish=2 <= j <= n_blk + 1, do_values=1 <= j <= n_blk,
              do_scores=j + 1 < n_blk, do_probs=0 <= j < n_blk)

    full = range(2, n_blk - 1)
    for j in range(-1, n_blk + 2):
        if j not in full:
            static_stage(j)
        elif j == full[0]:
            def body(i, carry):
                @pl.when(i % 2 == 0)
                def _():
                    stage(i, 0, True, True, True, True)

                @pl.when(i % 2 == 1)
                def _():
                    stage(i, 1, True, True, True, True)
                return carry
            lax.fori_loop(full[0], full[-1] + 1, body, 0)


def _attention(qT, k_lat, k_ctx, vT_lat, vT_ctx, lam_rows, subln_g, lam_init):
    b, _, n_blk, _, _ = qT.shape
    tq = n_blk * MXU_DIM
    tc = k_ctx.shape[2]
    has_lat = k_lat is not None
    tl = k_lat.shape[2] if has_lat else 0
    hs = ATTN_HEADS_PER_STEP
    head5 = lambda i, h: (i, h, 0, 0, 0)
    head4 = lambda i, h: (i, h, 0, 0)
    in_specs = [pl.BlockSpec((1, hs, n_blk, LANES, MXU_DIM), head5)]
    args = [qT]
    if has_lat:
        in_specs += [pl.BlockSpec((1, hs, tl, LANES), head4), pl.BlockSpec((1, hs, tc, LANES), head4),
                     pl.BlockSpec((1, hs, VT_ROWS, tl), head4), pl.BlockSpec((1, hs, VT_ROWS, tc), head4)]
        args += [k_lat, k_ctx, vT_lat, vT_ctx]
    else:
        in_specs += [pl.BlockSpec((1, hs, tc, LANES), head4), pl.BlockSpec((1, hs, VT_ROWS, tc), head4)]
        args += [k_ctx, vT_ctx]
    in_specs += [_const_spec((4, HEAD_DIM)), _const_spec((1, V_HEAD_DIM))]
    args += [lam_rows, subln_g]
    return pl.pallas_call(
        functools.partial(_attn_kernel, has_lat=has_lat, lam_init=lam_init),
        grid=(b, N_HEADS // hs),
        in_specs=in_specs,
        out_specs=pl.BlockSpec((1, hs, tq, V_HEAD_DIM), head4),
        out_shape=jax.ShapeDtypeStruct((b, N_HEADS, tq, V_HEAD_DIM), BF16),
        scratch_shapes=([pltpu.VMEM((ATTN_SLOTS, 8, MXU_DIM), F32),
                         pltpu.VMEM((ATTN_SLOTS, VT_ROWS, MXU_DIM), F32)]
                        + [pltpu.VMEM((tl + tc, MXU_DIM), F32)] * ATTN_SLOTS
                        + [pltpu.VMEM((tl + tc, MXU_DIM), BF16)] * ATTN_SLOTS),
        compiler_params=_params(2),
        name="diff_attention" if has_lat else "diff_attention_ctx",
    )(*args)


def _fourier_kernel(f_ref, wt_ref, cs_ref, o_ref, xcs_ref, *, scale, tr):
    t = f_ref.shape[1]
    cs = cs_ref[...]
    for g in range(N_FOURIER_GROUPS):
        cols = slice(g * FOURIER_GROUP_DIM, (g + 1) * FOURIER_GROUP_DIM)
        r = _dot(f_ref[0, :, cols], cs)
        xcs_ref[0:t, cols] = r[:, 0:FOURIER_GROUP_DIM].astype(BF16)
        xcs_ref[t:2 * t, cols] = r[:, FOURIER_GROUP_DIM:].astype(BF16)
    for i in range(t // tr):
        rows = slice(i * tr, (i + 1) * tr)
        o_ref[0, rows, :] = (_dot(wt_ref[rows, :], xcs_ref[...]) * scale).astype(BF16)


def _fourier(f, wt, cs):
    b, t, w = f.shape
    tr = min(t, 512)
    scale = 1.0 / math.sqrt(t * FOURIER_GROUP_DIM)
    return pl.pallas_call(
        functools.partial(_fourier_kernel, scale=scale, tr=tr),
        grid=(b,),
        in_specs=[
            pl.BlockSpec((1, t, w), lambda i: (i, 0, 0)),
            _const_spec((t, 2 * t)),
            _const_spec((FOURIER_GROUP_DIM, 2 * FOURIER_GROUP_DIM)),
        ],
        out_specs=pl.BlockSpec((1, t, w), lambda i: (i, 0, 0)),
        out_shape=jax.ShapeDtypeStruct((b, t, w), BF16),
        scratch_shapes=[pltpu.VMEM((2 * t, w), BF16)],
        compiler_params=_params(1),
        name="fourier_mix",
    )(f, wt, cs)


def _merge_ffn_kernel(o_ref, mix_ref, gate_ref, x_ref, mod_ref, g2_ref, wpa_ref, wpf_ref, wout_ref,
                      wgu_ref, wdn_ref, out_ref, act_ref):
    d = D_MODEL
    mod = mod_ref[0]
    o = jnp.concatenate([o_ref[0, hd] for hd in range(N_HEADS)], axis=-1)
    y_a = _dot(o, wpa_ref[...])
    y_f = _dot(mix_ref[0], wpf_ref[...])
    g_a = gate_ref[0, :, 0:d].astype(F32)
    g_f = gate_ref[0, :, d:2 * d].astype(F32)
    merged = (g_a * y_a + g_f * y_f).astype(BF16)
    x1 = x_ref[0] + mod[:, 2 * d:3 * d] * _dot(merged, wout_ref[...])

    h2 = _norm_modulate(x1, g2_ref[...], mod[:, 3 * d:4 * d], mod[:, 4 * d:5 * d]).astype(BF16)
    cw = MXU_DIM
    for c in range(D_FF // cw):
        gate = _dot(h2, wgu_ref[:, c * cw:(c + 1) * cw])
        up = _dot(h2, wgu_ref[:, D_FF + c * cw:D_FF + (c + 1) * cw])
        act_ref[:, c * cw:(c + 1) * cw] = (gate * _sigmoid(gate) * up).astype(BF16)
    out_ref[0] = x1 + mod[:, 5 * d:6 * d] * _dot(act_ref[...], wdn_ref[...])


def _merge_ffn(o, mix, gates, x, mod, per_batch_mod, g2, wpa, wpf, wout, wgu, wdn, *, tm):
    b, t, d = x.shape
    mod_map = (lambda i, j: (i, 0, 0)) if per_batch_mod else (lambda i, j: (0, 0, 0))
    tok = lambda i, j: (i, j, 0)
    return pl.pallas_call(
        _merge_ffn_kernel,
        grid=(b, t // tm),
        in_specs=[
            pl.BlockSpec((1, N_HEADS, tm, V_HEAD_DIM), lambda i, j: (i, 0, j, 0)),
            pl.BlockSpec((1, tm, FOURIER_WIDTH), tok),
            pl.BlockSpec((1, tm, 2 * d), tok),
            pl.BlockSpec((1, tm, d), tok),
            pl.BlockSpec((1, 1, N_MOD * d), mod_map),
            _const_spec((1, d)),
            _const_spec(wpa.shape), _const_spec(wpf.shape), _const_spec(wout.shape),
            _const_spec(wgu.shape), _const_spec(wdn.shape),
        ],
        out_specs=pl.BlockSpec((1, tm, d), tok),
        out_shape=jax.ShapeDtypeStruct((b, t, d), F32),
        scratch_shapes=[pltpu.VMEM((tm, D_FF), BF16)],
        compiler_params=_params(2),
        name="merge_ffn",
    )(o, mix, gates, x, mod, g2, wpa, wpf, wout, wgu, wdn)


def _rope_tables(n_tokens):
    rows = n_tokens // GRID_W
    r, col = np.meshgrid(np.arange(rows), np.arange(GRID_W), indexing="ij")
    r = r.reshape(-1).astype(np.float32)
    col = col.reshape(-1).astype(np.float32)
    axis_dim = HEAD_DIM // 2
    inv_freq = (ROPE_THETA ** (-np.arange(0, axis_dim, 2, dtype=np.float32) / axis_dim)).astype(np.float32)
    ang_r = r[:, None] * inv_freq[None, :]
    ang_c = col[:, None] * inv_freq[None, :]
    ang = np.concatenate([ang_r, ang_r, ang_c, ang_c], axis=-1)
    sign = np.where((np.arange(HEAD_DIM) % 32) < 16, -1.0, 1.0).astype(np.float32)
    cos = np.cos(ang).astype(np.float32)
    sin = (np.sin(ang) * sign).astype(np.float32)
    return np.tile(cos, (1, 2)), np.tile(sin, (1, 2))


def _dft_tables(n):
    kt = (np.arange(n)[:, None] * np.arange(n)[None, :]) % n
    ang = kt.astype(np.float64) * (2.0 * np.pi / n)
    return np.concatenate([np.cos(ang), -np.sin(ang)], axis=1).astype(np.float32)


def _channel_dft_table():
    n = FOURIER_GROUP_DIM
    kt = (np.arange(n)[:, None] * np.arange(n)[None, :]) % n
    ang = kt.astype(np.float64) * (2.0 * np.pi / n)
    return np.concatenate([np.cos(ang), np.sin(ang)], axis=1).astype(np.float32)


def _group_sum_matrix():
    g = np.arange(MXU_DIM) // HEAD_DIM
    return (g[:, None] == g[None, :]).astype(np.float32)


def kernel(x, c, ctx, c_ctx, w_ada, b_ada, norm1_g, norm2_g, w_in, q_norm_g, k_norm_g,
           lambda_q1, lambda_k1, lambda_q2, lambda_k2, subln_g, w_proj_attn, w_proj_fourier,
           w_out, w_gate_up, w_down):
    b, t, d = x.shape
    tc = ctx.shape[1]
    depth = w_ada.shape[0]
    assert b + 1 <= MOD_ROWS and d == D_MODEL

    cc = jnp.zeros((MOD_ROWS, d), F32).at[:b].set(c).at[b].set(c_ctx)
    mod = _modulation(cc, w_ada, b_ada)

    cos_np, sin_np = _rope_tables(t)
    cos_lat, sin_lat = jnp.asarray(cos_np), jnp.asarray(sin_np)
    cos_ctx = jnp.ones((tc, LANES), F32)
    sin_ctx = jnp.zeros((tc, LANES), F32)
    wt_lat = jnp.asarray(_dft_tables(t)).astype(BF16)
    wt_ctx = jnp.asarray(_dft_tables(tc)).astype(BF16)
    cs = jnp.asarray(_channel_dft_table()).astype(BF16)
    bd = jnp.asarray(_group_sum_matrix()).astype(BF16)

    cx = ctx
    for l in range(depth):
        last = l == depth - 1
        lam_init = 0.8 - 0.6 * math.exp(-0.3 * l)
        mod_lat = mod[l, :b][:, None, :]
        mod_ctx = mod[l, b:b + 1][None]
        g1 = norm1_g[l][None]
        g2 = norm2_g[l][None]
        qg = jnp.tile(q_norm_g[l], 2)[None]
        kg = jnp.tile(k_norm_g[l], 2)[None]
        w_in_l = w_in[l].astype(BF16)
        lam_rows = jnp.stack([lambda_q1[l], lambda_k1[l], lambda_q2[l], lambda_k2[l]])
        sg = subln_g[l][None]
        weights = (w_proj_attn[l].astype(BF16), w_proj_fourier[l].astype(BF16), w_out[l].astype(BF16),
                   w_gate_up[l].astype(BF16), w_down[l].astype(BF16))

        if last:
            k_ctx, vT_ctx = _inproj(cx, mod_ctx, False, g1, w_in_l[:, :KV_COLS], cos_ctx, sin_ctx,
                                    qg, kg, bd, tm=tc, kv_only=True)
        else:
            qT_ctx, k_ctx, vT_ctx, f_ctx, gates_ctx = _inproj(
                cx, mod_ctx, False, g1, w_in_l, cos_ctx, sin_ctx, qg, kg, bd, tm=tc, kv_only=False)
        qT, k_lat, vT_lat, f_lat, gates = _inproj(
            x, mod_lat, True, g1, w_in_l, cos_lat, sin_lat, qg, kg, bd, tm=512, kv_only=False)

        o = _attention(qT, k_lat, k_ctx, vT_lat, vT_ctx, lam_rows, sg, lam_init)
        mix = _fourier(f_lat, wt_lat, cs)
        x = _merge_ffn(o, mix, gates, x, mod_lat, True, g2, *weights, tm=256)

        if not last:
            o_ctx = _attention(qT_ctx, None, k_ctx, None, vT_ctx, lam_rows, sg, lam_init)
            mix_ctx = _fourier(f_ctx, wt_ctx, cs)
            cx = _merge_ffn(o_ctx, mix_ctx, gates_ctx, cx, mod_ctx, False, g2, *weights, tm=tc)
    return x
```

```python
import functools
import math

import jax
import jax.numpy as jnp
import numpy as np
from jax import lax
from jax.experimental import pallas as pl
from jax.experimental.pallas import tpu as pltpu

F32 = jnp.float32
BF16 = jnp.bfloat16

D_MODEL = 1024
N_HEADS = 8
HEAD_DIM = 64
V_HEAD_DIM = 2 * HEAD_DIM
ATTN_WIDTH = N_HEADS * V_HEAD_DIM
N_FOURIER_GROUPS = 4
FOURIER_GROUP_DIM = 128
FOURIER_WIDTH = N_FOURIER_GROUPS * FOURIER_GROUP_DIM
KV_COLS = 2 * ATTN_WIDTH
Q_COL0 = KV_COLS
F_COL0 = 3 * ATTN_WIDTH
G_COL0 = F_COL0 + FOURIER_WIDTH
IN_WIDTH = G_COL0 + 2 * D_MODEL
D_FF = 2816
N_MOD = 6
GRID_W = 64
ROPE_THETA = 10000.0
EPS = 1e-6

LANES = 128
MXU_DIM = 256
ONES_ROWS = 16
VT_ROWS = V_HEAD_DIM + ONES_ROWS
MOD_ROWS = 24
VMEM_LIMIT = 56 * 1024 * 1024
LOG2E = 1.4426950408889634
ATTN_BLOCKS_PER_STAGE = 1
ATTN_SLOTS = 4 * ATTN_BLOCKS_PER_STAGE
ATTN_HEADS_PER_STEP = 4


def _params(n_axes):
    return pltpu.CompilerParams(dimension_semantics=("arbitrary",) * n_axes,
                                vmem_limit_bytes=VMEM_LIMIT)


def _const_spec(shape):
    nd = len(shape)
    return pl.BlockSpec(shape, lambda *_: (0,) * nd, pipeline_mode=pl.Buffered(1))


def _dot(a, b):
    return jnp.dot(a, b, preferred_element_type=F32)


def _sigmoid(x):
    return 1.0 / (1.0 + jnp.exp(-x))


def _mod_kernel(c_ref, w_ref, b_ref, o_ref):
    c = c_ref[...]
    s = c * _sigmoid(c)
    s_hi = s.astype(BF16)
    s_lo = (s - s_hi.astype(F32)).astype(BF16)
    w = w_ref[0]
    w_hi = w.astype(BF16)
    w_lo = (w - w_hi.astype(F32)).astype(BF16)
    acc = _dot(s_hi, w_hi) + _dot(s_hi, w_lo) + _dot(s_lo, w_hi)
    o_ref[0] = acc + b_ref[0]


def _modulation(cc, w_ada, b_ada):
    depth, d, n = w_ada.shape
    tn = 1024
    return pl.pallas_call(
        _mod_kernel,
        grid=(depth, n // tn),
        in_specs=[
            pl.BlockSpec((MOD_ROWS, d), lambda l, j: (0, 0)),
            pl.BlockSpec((1, d, tn), lambda l, j: (l, 0, j)),
            pl.BlockSpec((1, 1, tn), lambda l, j: (l, 0, j)),
        ],
        out_specs=pl.BlockSpec((1, MOD_ROWS, tn), lambda l, j: (l, 0, j)),
        out_shape=jax.ShapeDtypeStruct((depth, MOD_ROWS, n), F32),
        compiler_params=_params(2),
        name="adaln_modulation",
    )(cc, w_ada, b_ada.reshape(depth, 1, n))


def _norm_modulate(x, g, shift, scale):
    ms = jnp.mean(x * x, axis=-1, keepdims=True)
    return (x * lax.rsqrt(ms + EPS) * g) * (1.0 + scale) + shift


def _inproj_kernel(x_ref, mod_ref, g1_ref, w_ref, cos_ref, sin_ref, qg_ref, kg_ref, bd_ref,
                   *out_refs, kv_only):
    if kv_only:
        k_ref, vT_ref = out_refs
    else:
        qT_ref, k_ref, vT_ref, f_ref, gate_ref = out_refs
    tm = x_ref.shape[1]
    mod = mod_ref[0]
    h = _norm_modulate(x_ref[0], g1_ref[...], mod[:, 0:D_MODEL], mod[:, D_MODEL:2 * D_MODEL])
    hb = h.astype(BF16)
    cos = cos_ref[...]
    sin = sin_ref[...]
    lane = lax.broadcasted_iota(jnp.int32, (tm, LANES), 1)
    first_half = jnp.bitwise_and(lane, 31) < 16
    bd = bd_ref[...]

    def normed_rope(p, gain):
        ssq = _dot((p * p).astype(BF16), bd)
        y = p * lax.rsqrt(ssq * (1.0 / HEAD_DIM) + EPS)
        slabs = []
        for s in range(MXU_DIM // LANES):
            ys = y[:, s * LANES:(s + 1) * LANES] * gain
            rot = jnp.where(first_half, pltpu.roll(ys, LANES - 16, 1), pltpu.roll(ys, 16, 1))
            slabs.append(ys * cos + rot * sin)
        return slabs

    kg = kg_ref[...]
    ones = jnp.ones((ONES_ROWS, tm), BF16)

    def finish_k(p, c):
        for s, slab in enumerate(normed_rope(p, kg)):
            k_ref[0, 2 * c + s] = slab.astype(BF16)

    def finish_v(v, c):
        for s in range(MXU_DIM // LANES):
            hd = 2 * c + s
            vT_ref[0, hd, 0:V_HEAD_DIM, :] = v[:, s * LANES:(s + 1) * LANES].T.astype(BF16)
            vT_ref[0, hd, V_HEAD_DIM:VT_ROWS, :] = ones

    def finish_q(p, c):
        qg = qg_ref[...] * (HEAD_DIM ** -0.5 * LOG2E)
        for s, slab in enumerate(normed_rope(p, qg)):
            st = slab.T.astype(BF16)
            for j in range(tm // MXU_DIM):
                qT_ref[0, 2 * c + s, j] = st[:, j * MXU_DIM:(j + 1) * MXU_DIM]

    def finish_f(f, c):
        f_ref[0, :, c * MXU_DIM:(c + 1) * MXU_DIM] = f.astype(BF16)

    def finish_gate(g, c):
        gate_ref[0, :, c * MXU_DIM:(c + 1) * MXU_DIM] = _sigmoid(g).astype(BF16)

    sections = [(0, ATTN_WIDTH, finish_k), (ATTN_WIDTH, ATTN_WIDTH, finish_v)]
    if not kv_only:
        sections += [(Q_COL0, ATTN_WIDTH, finish_q), (F_COL0, FOURIER_WIDTH, finish_f),
                     (G_COL0, 2 * D_MODEL, finish_gate)]
    pending = None
    for col0, width, finish in sections:
        for c in range(width // MXU_DIM):
            p = _dot(hb, w_ref[:, col0 + c * MXU_DIM:col0 + (c + 1) * MXU_DIM])
            if pending is not None:
                pending[0](pending[1], pending[2])
            pending = (finish, p, c)
    pending[0](pending[1], pending[2])


def _inproj(x, mod, per_batch_mod, g1, w, cos, sin, qg, kg, bd, *, tm, kv_only):
    b, t, d = x.shape
    nq = tm // MXU_DIM
    n_w = w.shape[1]
    mod_map = (lambda i, j: (i, 0, 0)) if per_batch_mod else (lambda i, j: (0, 0, 0))
    in_specs = [
        pl.BlockSpec((1, tm, d), lambda i, j: (i, j, 0)),
        pl.BlockSpec((1, 1, N_MOD * d), mod_map),
        _const_spec((1, d)),
        _const_spec((d, n_w)),
        pl.BlockSpec((tm, LANES), lambda i, j: (j, 0)),
        pl.BlockSpec((tm, LANES), lambda i, j: (j, 0)),
        _const_spec((1, LANES)),
        _const_spec((1, LANES)),
        _const_spec((MXU_DIM, MXU_DIM)),
    ]
    k_spec = pl.BlockSpec((1, N_HEADS, tm, LANES), lambda i, j: (i, 0, j, 0))
    vT_spec = pl.BlockSpec((1, N_HEADS, VT_ROWS, tm), lambda i, j: (i, 0, 0, j))
    k_shape = jax.ShapeDtypeStruct((b, N_HEADS, t, LANES), BF16)
    vT_shape = jax.ShapeDtypeStruct((b, N_HEADS, VT_ROWS, t), BF16)
    if kv_only:
        out_specs = [k_spec, vT_spec]
        out_shape = [k_shape, vT_shape]
    else:
        out_specs = [
            pl.BlockSpec((1, N_HEADS, nq, LANES, MXU_DIM), lambda i, j: (i, 0, j, 0, 0)),
            k_spec, vT_spec,
            pl.BlockSpec((1, tm, FOURIER_WIDTH), lambda i, j: (i, j, 0)),
            pl.BlockSpec((1, tm, 2 * d), lambda i, j: (i, j, 0)),
        ]
        out_shape = [
            jax.ShapeDtypeStruct((b, N_HEADS, t // MXU_DIM, LANES, MXU_DIM), BF16),
            k_shape, vT_shape,
            jax.ShapeDtypeStruct((b, t, FOURIER_WIDTH), BF16),
            jax.ShapeDtypeStruct((b, t, 2 * d), BF16),
        ]
    return pl.pallas_call(
        functools.partial(_inproj_kernel, kv_only=kv_only),
        grid=(b, t // tm),
        in_specs=in_specs,
        out_specs=out_specs,
        out_shape=out_shape,
        compiler_params=_params(2),
        name="inproj_kv" if kv_only else "inproj",
    )(x, mod, g1, w, cos, sin, qg, kg, bd)


def _attn_kernel(*refs, has_lat, lam_init):
    if has_lat:
        qT_ref, kl_ref, kc_ref, vTl_ref, vTc_ref, lam_ref, sg_ref, o_ref = refs[:8]
        tl = kl_ref.shape[2]
    else:
        qT_ref, kc_ref, vTc_ref, lam_ref, sg_ref, o_ref = refs[:6]
        tl = 0
    m_ref, acc_ref = refs[-2 * ATTN_SLOTS - 2], refs[-2 * ATTN_SLOTS - 1]
    s_refs = refs[-2 * ATTN_SLOTS:-ATTN_SLOTS]
    p_refs = refs[-ATTN_SLOTS:]
    tc = kc_ref.shape[2]
    n_heads, nb = qT_ref.shape[1], qT_ref.shape[2]
    n_blk = n_heads * nb
    ch = MXU_DIM

    def split(j):
        if isinstance(j, int):
            return j // nb, j % nb
        assert nb & (nb - 1) == 0
        return lax.shift_right_logical(j, nb.bit_length() - 1), lax.bitwise_and(j, nb - 1)
    lp = lam_ref[...]
    lam = (jnp.exp(jnp.sum(lp[0:1] * lp[1:2], axis=-1, keepdims=True))
           - jnp.exp(jnp.sum(lp[2:3] * lp[3:4], axis=-1, keepdims=True)) + lam_init)
    out_gain = sg_ref[...] * (1.0 - lam_init)
    row = lax.broadcasted_iota(jnp.int32, (V_HEAD_DIM, MXU_DIM), 0)
    zero = jnp.zeros((V_HEAD_DIM, MXU_DIM), BF16)

    chunks = ([(c * ch, kl_ref, c * ch) for c in range(tl // ch)]
              + [(tl + c * ch, kc_ref, c * ch) for c in range(tc // ch)])

    def score_chunk(slot, w, chunk, head):
        dst, k_ref, src = chunk
        sc = jnp.dot(k_ref[0, head, src:src + ch, :], w, preferred_element_type=F32)
        s_refs[slot][dst:dst + ch, :] = sc
        return jnp.max(sc, axis=0, keepdims=True)

    def prob_chunk(slot, m, chunk):
        dst = chunk[0]
        p_refs[slot][dst:dst + ch, :] = jnp.exp2(s_refs[slot][dst:dst + ch, :] - m).astype(BF16)

    def weighted_values(slot, head):
        p_ref = p_refs[slot]
        a = jnp.dot(vTc_ref[0, head], p_ref[tl:, :], preferred_element_type=F32)
        if has_lat:
            a = a + jnp.dot(vTl_ref[0, head], p_ref[0:tl, :], preferred_element_type=F32)
        return a

    def finish_block(j, acc1, acc2):
        head, blk = split(j)
        o1 = acc1[0:V_HEAD_DIM] * (1.0 / acc1[V_HEAD_DIM:V_HEAD_DIM + 1])
        o2 = acc2[0:V_HEAD_DIM] * (1.0 / acc2[V_HEAD_DIM:V_HEAD_DIM + 1])
        o = o1 - lam * o2
        ms = jnp.sum(o * o, axis=0, keepdims=True) * (1.0 / V_HEAD_DIM)
        on = (o * lax.rsqrt(ms + EPS)).T * out_gain
        if isinstance(blk, int):
            rows = slice(blk * MXU_DIM, (blk + 1) * MXU_DIM)
        else:
            rows = pl.ds(pl.multiple_of(blk * MXU_DIM, MXU_DIM), MXU_DIM)
        o_ref[0, head, rows, :] = on.astype(BF16)

    grp = ATTN_BLOCKS_PER_STAGE
    assert n_blk % grp == 0
    n_units = n_blk // grp

    def slot(par, g, half):
        return (par * grp + g) * 2 + half

    def stage(u, par, do_finish, do_values, do_scores, do_probs):
        other = 1 - par
        gs = range(grp)
        if do_finish:
            for g in gs:
                finish_block(grp * (u - 2) + g, acc_ref[slot(par, g, 0)], acc_ref[slot(par, g, 1)])
        if do_values:
            v_head = [split(grp * (u - 1) + g)[0] for g in gs]
        if do_scores:
            s_head, w = [], []
            for g in gs:
                head, blk = split(grp * (u + 1) + g)
                qT = qT_ref[0, head, blk]
                s_head.append(head)
                w.append((jnp.where(row < HEAD_DIM, qT, zero), jnp.where(row >= HEAD_DIM, qT, zero)))
        if do_probs:
            pm = [[m_ref[slot(par, g, half), 0:1, :] for half in range(2)] for g in gs]
        mx = [[None, None] for _ in gs]
        for chunk in chunks:
            for g in gs:
                for half in range(2):
                    if do_scores:
                        c = score_chunk(slot(other, g, half), w[g][half], chunk, s_head[g])
                        mx[g][half] = c if mx[g][half] is None else jnp.maximum(mx[g][half], c)
            for g in gs:
                for half in range(2):
                    if do_probs:
                        prob_chunk(slot(par, g, half), pm[g][half], chunk)
        for g in gs:
            for half in range(2):
                if do_scores:
                    m_ref[slot(other, g, half)] = jnp.broadcast_to(mx[g][half], (8, MXU_DIM))
                if do_values:
                    acc_ref[slot(other, g, half)] = weighted_values(slot(other, g, half), v_head[g])

    def static_stage(u):
        stage(u, u % 2, do_finish=2 <= u <= n_units + 1, do_values=1 <= u <= n_units,
              do_scores=u + 1 < n_units, do_probs=0 <= u < n_units)

    full = range(2, n_units - 1)
    for j in range(-1, n_units + 2):
        if j not in full:
            static_stage(j)
        elif j == full[0]:
            def body(i, carry):
                @pl.when(i % 2 == 0)
                def _():
                    stage(i, 0, True, True, True, True)

                @pl.when(i % 2 == 1)
                def _():
                    stage(i, 1, True, True, True, True)
                return carry
            lax.fori_loop(full[0], full[-1] + 1, body, 0)


def _attention(qT, k_lat, k_ctx, vT_lat, vT_ctx, lam_rows, subln_g, lam_init):
    b, _, n_blk, _, _ = qT.shape
    tq = n_blk * MXU_DIM
    tc = k_ctx.shape[2]
    has_lat = k_lat is not None
    tl = k_lat.shape[2] if has_lat else 0
    hs = ATTN_HEADS_PER_STEP
    head5 = lambda i, h: (i, h, 0, 0, 0)
    head4 = lambda i, h: (i, h, 0, 0)
    in_specs = [pl.BlockSpec((1, hs, n_blk, LANES, MXU_DIM), head5)]
    args = [qT]
    if has_lat:
        in_specs += [pl.BlockSpec((1, hs, tl, LANES), head4), pl.BlockSpec((1, hs, tc, LANES), head4),
                     pl.BlockSpec((1, hs, VT_ROWS, tl), head4), pl.BlockSpec((1, hs, VT_ROWS, tc), head4)]
        args += [k_lat, k_ctx, vT_lat, vT_ctx]
    else:
        in_specs += [pl.BlockSpec((1, hs, tc, LANES), head4), pl.BlockSpec((1, hs, VT_ROWS, tc), head4)]
        args += [k_ctx, vT_ctx]
    in_specs += [_const_spec((4, HEAD_DIM)), _const_spec((1, V_HEAD_DIM))]
    args += [lam_rows, subln_g]
    return pl.pallas_call(
        functools.partial(_attn_kernel, has_lat=has_lat, lam_init=lam_init),
        grid=(b, N_HEADS // hs),
        in_specs=in_specs,
        out_specs=pl.BlockSpec((1, hs, tq, V_HEAD_DIM), head4),
        out_shape=jax.ShapeDtypeStruct((b, N_HEADS, tq, V_HEAD_DIM), BF16),
        scratch_shapes=([pltpu.VMEM((ATTN_SLOTS, 8, MXU_DIM), F32),
                         pltpu.VMEM((ATTN_SLOTS, VT_ROWS, MXU_DIM), F32)]
                        + [pltpu.VMEM((tl + tc, MXU_DIM), F32)] * ATTN_SLOTS
                        + [pltpu.VMEM((tl + tc, MXU_DIM), BF16)] * ATTN_SLOTS),
        compiler_params=_params(2),
        name="diff_attention" if has_lat else "diff_attention_ctx",
    )(*args)


def _fourier_kernel(f_ref, ch_ref, sh_ref, cs_ref, rev_ref, o_ref, xc_ref, xs_ref, y_ref, *, scale):
    t = f_ref.shape[1]
    half = t // 2
    cs = cs_ref[...]
    for g in range(N_FOURIER_GROUPS):
        cols = slice(g * FOURIER_GROUP_DIM, (g + 1) * FOURIER_GROUP_DIM)
        r = _dot(f_ref[0, :, cols], cs)
        xc_ref[:, cols] = r[:, 0:FOURIER_GROUP_DIM].astype(BF16)
        xs_ref[:, cols] = r[:, FOURIER_GROUP_DIM:].astype(BF16)
    n_rows = ch_ref.shape[0]
    tr = min(512, half)
    for r0 in range(0, n_rows, tr):
        rows = slice(r0, min(r0 + tr, n_rows))
        a = _dot(ch_ref[rows, :], xc_ref[...])
        b = _dot(sh_ref[rows, :], xs_ref[...])
        if r0 < half:
            o_ref[0, rows, :] = ((a - b) * scale).astype(BF16)
        y_ref[rows, :] = ((a + b) * scale).astype(BF16)
    rb = rev_ref.shape[0]
    win = rev_ref.shape[1]
    for blk in range(half // rb):
        src = half - rb * (blk + 1)
        o_ref[0, half + blk * rb:half + (blk + 1) * rb, :] = _dot(
            rev_ref[...], y_ref[src:src + win, :]).astype(BF16)


def _fourier(f, ch, sh, cs, rev):
    b, t, w = f.shape
    scale = 1.0 / math.sqrt(t * FOURIER_GROUP_DIM)
    n_rows = ch.shape[0]
    return pl.pallas_call(
        functools.partial(_fourier_kernel, scale=scale),
        grid=(b,),
        in_specs=[
            pl.BlockSpec((1, t, w), lambda i: (i, 0, 0)),
            _const_spec(ch.shape), _const_spec(sh.shape),
            _const_spec((FOURIER_GROUP_DIM, 2 * FOURIER_GROUP_DIM)),
            _const_spec(rev.shape),
        ],
        out_specs=pl.BlockSpec((1, t, w), lambda i: (i, 0, 0)),
        out_shape=jax.ShapeDtypeStruct((b, t, w), BF16),
        scratch_shapes=[pltpu.VMEM((t, w), BF16), pltpu.VMEM((t, w), BF16), pltpu.VMEM((n_rows, w), BF16)],
        compiler_params=_params(1),
        name="fourier_mix",
    )(f, ch, sh, cs, rev)


def _merge_ffn_kernel(o_ref, mix_ref, gate_ref, x_ref, mod_ref, g2_ref, wpa_ref, wpf_ref, wout_ref,
                      wgu_ref, wdn_ref, out_ref, act_ref):
    d = D_MODEL
    mod = mod_ref[0]
    tm = x_ref.shape[1]
    subs = [slice(r, r + MXU_DIM) for r in range(0, tm, MXU_DIM)]

    branches = []
    for rows in subs:
        o = jnp.concatenate([o_ref[0, hd, rows, :] for hd in range(N_HEADS)], axis=-1)
        branches.append((_dot(o, wpa_ref[...]), _dot(mix_ref[0, rows, :], wpf_ref[...])))
    projected = []
    for rows, (y_a, y_f) in zip(subs, branches):
        g_a = gate_ref[0, rows, 0:d].astype(F32)
        g_f = gate_ref[0, rows, d:2 * d].astype(F32)
        projected.append(_dot((g_a * y_a + g_f * y_f).astype(BF16), wout_ref[...]))
    x1s, h2s = [], []
    for rows, out in zip(subs, projected):
        x1 = x_ref[0, rows, :] + mod[:, 2 * d:3 * d] * out
        x1s.append(x1)
        h2s.append(_norm_modulate(x1, g2_ref[...], mod[:, 3 * d:4 * d], mod[:, 4 * d:5 * d]).astype(BF16))
    cw = MXU_DIM
    for c in range(D_FF // cw):
        for rows, h2 in zip(subs, h2s):
            gate = _dot(h2, wgu_ref[:, c * cw:(c + 1) * cw])
            up = _dot(h2, wgu_ref[:, D_FF + c * cw:D_FF + (c + 1) * cw])
            act_ref[rows, c * cw:(c + 1) * cw] = (gate * _sigmoid(gate) * up).astype(BF16)
    for rows, x1 in zip(subs, x1s):
        out_ref[0, rows, :] = x1 + mod[:, 5 * d:6 * d] * _dot(act_ref[rows, :], wdn_ref[...])


def _merge_ffn(o, mix, gates, x, mod, per_batch_mod, g2, wpa, wpf, wout, wgu, wdn, *, tm):
    b, t, d = x.shape
    mod_map = (lambda i, j: (i, 0, 0)) if per_batch_mod else (lambda i, j: (0, 0, 0))
    tok = lambda i, j: (i, j, 0)
    return pl.pallas_call(
        _merge_ffn_kernel,
        grid=(b, t // tm),
        in_specs=[
            pl.BlockSpec((1, N_HEADS, tm, V_HEAD_DIM), lambda i, j: (i, 0, j, 0)),
            pl.BlockSpec((1, tm, FOURIER_WIDTH), tok),
            pl.BlockSpec((1, tm, 2 * d), tok),
            pl.BlockSpec((1, tm, d), tok),
            pl.BlockSpec((1, 1, N_MOD * d), mod_map),
            _const_spec((1, d)),
            _const_spec(wpa.shape), _const_spec(wpf.shape), _const_spec(wout.shape),
            _const_spec(wgu.shape), _const_spec(wdn.shape),
        ],
        out_specs=pl.BlockSpec((1, tm, d), tok),
        out_shape=jax.ShapeDtypeStruct((b, t, d), F32),
        scratch_shapes=[pltpu.VMEM((tm, D_FF), BF16)],
        compiler_params=_params(2),
        name="merge_ffn",
    )(o, mix, gates, x, mod, g2, wpa, wpf, wout, wgu, wdn)


def _rope_tables(n_tokens):
    rows = n_tokens // GRID_W
    r, col = np.meshgrid(np.arange(rows), np.arange(GRID_W), indexing="ij")
    r = r.reshape(-1).astype(np.float32)
    col = col.reshape(-1).astype(np.float32)
    axis_dim = HEAD_DIM // 2
    inv_freq = (ROPE_THETA ** (-np.arange(0, axis_dim, 2, dtype=np.float32) / axis_dim)).astype(np.float32)
    ang_r = r[:, None] * inv_freq[None, :]
    ang_c = col[:, None] * inv_freq[None, :]
    ang = np.concatenate([ang_r, ang_r, ang_c, ang_c], axis=-1)
    sign = np.where((np.arange(HEAD_DIM) % 32) < 16, -1.0, 1.0).astype(np.float32)
    cos = np.cos(ang).astype(np.float32)
    sin = (np.sin(ang) * sign).astype(np.float32)
    return np.tile(cos, (1, 2)), np.tile(sin, (1, 2))


BF16_SUBLANES = 16


def _dft_tables(n):
    half = n // 2
    rows = half + BF16_SUBLANES
    kt = (np.arange(half + 1)[:, None] * np.arange(n)[None, :]) % n
    ang = kt.astype(np.float64) * (2.0 * np.pi / n)
    cos = np.zeros((rows, n), np.float32)
    sin = np.zeros((rows, n), np.float32)
    cos[:half + 1] = np.cos(ang)
    sin[:half + 1] = np.sin(ang)
    return cos, sin


def _row_reversal(n):
    rb = min(MXU_DIM, n // 2)
    rev = np.zeros((rb, rb + BF16_SUBLANES), np.float32)
    rev[np.arange(rb), rb - np.arange(rb)] = 1.0
    return rev


def _channel_dft_table():
    n = FOURIER_GROUP_DIM
    kt = (np.arange(n)[:, None] * np.arange(n)[None, :]) % n
    ang = kt.astype(np.float64) * (2.0 * np.pi / n)
    return np.concatenate([np.cos(ang), np.sin(ang)], axis=1).astype(np.float32)


def _group_sum_matrix():
    g = np.arange(MXU_DIM) // HEAD_DIM
    return (g[:, None] == g[None, :]).astype(np.float32)


def kernel(x, c, ctx, c_ctx, w_ada, b_ada, norm1_g, norm2_g, w_in, q_norm_g, k_norm_g,
           lambda_q1, lambda_k1, lambda_q2, lambda_k2, subln_g, w_proj_attn, w_proj_fourier,
           w_out, w_gate_up, w_down):
    b, t, d = x.shape
    tc = ctx.shape[1]
    depth = w_ada.shape[0]
    assert b + 1 <= MOD_ROWS and d == D_MODEL

    cc = jnp.zeros((MOD_ROWS, d), F32).at[:b].set(c).at[b].set(c_ctx)
    mod = _modulation(cc, w_ada, b_ada)

    cos_np, sin_np = _rope_tables(t)
    cos_lat, sin_lat = jnp.asarray(cos_np), jnp.asarray(sin_np)
    cos_ctx = jnp.ones((tc, LANES), F32)
    sin_ctx = jnp.zeros((tc, LANES), F32)
    dft_lat = [jnp.asarray(a).astype(BF16) for a in (*_dft_tables(t), _row_reversal(t))]
    dft_ctx = [jnp.asarray(a).astype(BF16) for a in (*_dft_tables(tc), _row_reversal(tc))]
    cs = jnp.asarray(_channel_dft_table()).astype(BF16)
    bd = jnp.asarray(_group_sum_matrix()).astype(BF16)

    cx = ctx
    for l in range(depth):
        last = l == depth - 1
        lam_init = 0.8 - 0.6 * math.exp(-0.3 * l)
        mod_lat = mod[l, :b][:, None, :]
        mod_ctx = mod[l, b:b + 1][None]
        g1 = norm1_g[l][None]
        g2 = norm2_g[l][None]
        qg = jnp.tile(q_norm_g[l], 2)[None]
        kg = jnp.tile(k_norm_g[l], 2)[None]
        w_in_l = w_in[l].astype(BF16)
        lam_rows = jnp.stack([lambda_q1[l], lambda_k1[l], lambda_q2[l], lambda_k2[l]])
        sg = subln_g[l][None]
        weights = (w_proj_attn[l].astype(BF16), w_proj_fourier[l].astype(BF16), w_out[l].astype(BF16),
                   w_gate_up[l].astype(BF16), w_down[l].astype(BF16))

        if last:
            k_ctx, vT_ctx = _inproj(cx, mod_ctx, False, g1, w_in_l[:, :KV_COLS], cos_ctx, sin_ctx,
                                    qg, kg, bd, tm=tc, kv_only=True)
        else:
            qT_ctx, k_ctx, vT_ctx, f_ctx, gates_ctx = _inproj(
                cx, mod_ctx, False, g1, w_in_l, cos_ctx, sin_ctx, qg, kg, bd, tm=tc, kv_only=False)
        qT, k_lat, vT_lat, f_lat, gates = _inproj(
            x, mod_lat, True, g1, w_in_l, cos_lat, sin_lat, qg, kg, bd, tm=512, kv_only=False)

        o = _attention(qT, k_lat, k_ctx, vT_lat, vT_ctx, lam_rows, sg, lam_init)
        mix = _fourier(f_lat, dft_lat[0], dft_lat[1], cs, dft_lat[2])
        x = _merge_ffn(o, mix, gates, x, mod_lat, True, g2, *weights, tm=512)

        if not last:
            o_ctx = _attention(qT_ctx, None, k_ctx, None, vT_ctx, lam_rows, sg, lam_init)
            mix_ctx = _fourier(f_ctx, dft_ctx[0], dft_ctx[1], cs, dft_ctx[2])
            cx = _merge_ffn(o_ctx, mix_ctx, gates_ctx, cx, mod_ctx, False, g2, *weights, tm=tc)
    return x
```

```python
import functools
import math

import jax
import jax.numpy as jnp
import numpy as np
from jax import lax
from jax.experimental import pallas as pl
from jax.experimental.pallas import tpu as pltpu

F32 = jnp.float32
BF16 = jnp.bfloat16

D_MODEL = 1024
N_HEADS = 8
HEAD_DIM = 64
V_HEAD_DIM = 2 * HEAD_DIM
ATTN_WIDTH = N_HEADS * V_HEAD_DIM
N_FOURIER_GROUPS = 4
FOURIER_GROUP_DIM = 128
FOURIER_WIDTH = N_FOURIER_GROUPS * FOURIER_GROUP_DIM
KV_COLS = 2 * ATTN_WIDTH
Q_COL0 = KV_COLS
F_COL0 = 3 * ATTN_WIDTH
G_COL0 = F_COL0 + FOURIER_WIDTH
IN_WIDTH = G_COL0 + 2 * D_MODEL
D_FF = 2816
N_MOD = 6
GRID_W = 64
ROPE_THETA = 10000.0
EPS = 1e-6

LANES = 128
MXU_DIM = 256
ONES_ROWS = 16
VT_ROWS = V_HEAD_DIM + ONES_ROWS
MOD_ROWS = 24
VMEM_LIMIT = 56 * 1024 * 1024
LOG2E = 1.4426950408889634
ATTN_BLOCKS_PER_STAGE = 1
ATTN_SLOTS = 4 * ATTN_BLOCKS_PER_STAGE
ATTN_HEADS_PER_STEP = 8


def _params(n_axes):
    return pltpu.CompilerParams(dimension_semantics=("arbitrary",) * n_axes,
                                vmem_limit_bytes=VMEM_LIMIT)


def _const_spec(shape):
    nd = len(shape)
    return pl.BlockSpec(shape, lambda *_: (0,) * nd, pipeline_mode=pl.Buffered(1))


def _layer_spec(rows, cols, layer):
    return pl.BlockSpec((None, rows, cols), lambda *_: (layer, 0, 0), pipeline_mode=pl.Buffered(1))


def _dot(a, b):
    return jnp.dot(a, b, preferred_element_type=F32)


def _sigmoid(x):
    return 1.0 / (1.0 + jnp.exp(-x))


def _mod_kernel(c_ref, w_ref, b_ref, o_ref):
    c = c_ref[...]
    s = c * _sigmoid(c)
    s_hi = s.astype(BF16)
    s_lo = (s - s_hi.astype(F32)).astype(BF16)
    w = w_ref[0]
    w_hi = w.astype(BF16)
    w_lo = (w - w_hi.astype(F32)).astype(BF16)
    acc = _dot(s_hi, w_hi) + _dot(s_hi, w_lo) + _dot(s_lo, w_hi)
    o_ref[0] = acc + b_ref[0]


def _modulation(cc, w_ada, b_ada):
    depth, d, n = w_ada.shape
    tn = 1024
    return pl.pallas_call(
        _mod_kernel,
        grid=(depth, n // tn),
        in_specs=[
            pl.BlockSpec((MOD_ROWS, d), lambda l, j: (0, 0)),
            pl.BlockSpec((1, d, tn), lambda l, j: (l, 0, j)),
            pl.BlockSpec((1, 1, tn), lambda l, j: (l, 0, j)),
        ],
        out_specs=pl.BlockSpec((1, MOD_ROWS, tn), lambda l, j: (l, 0, j)),
        out_shape=jax.ShapeDtypeStruct((depth, MOD_ROWS, n), F32),
        compiler_params=_params(2),
        name="adaln_modulation",
    )(cc, w_ada, b_ada.reshape(depth, 1, n))


def _norm_modulate(x, g, shift, scale):
    ms = jnp.mean(x * x, axis=-1, keepdims=True)
    return (x * lax.rsqrt(ms + EPS) * g) * (1.0 + scale) + shift


def _inproj_kernel(x_ref, mod_ref, g1_ref, w_ref, cos_ref, sin_ref, qg_ref, kg_ref, bd_ref,
                   *out_refs, kv_only):
    if kv_only:
        k_ref, vT_ref = out_refs
    else:
        qT_ref, k_ref, vT_ref, f_ref, gate_ref = out_refs
    tm = x_ref.shape[1]
    mod = mod_ref[0]
    h = _norm_modulate(x_ref[0], g1_ref[...], mod[:, 0:D_MODEL], mod[:, D_MODEL:2 * D_MODEL])
    hb = h.astype(BF16)
    cos = cos_ref[...]
    sin = sin_ref[...]
    lane = lax.broadcasted_iota(jnp.int32, (tm, LANES), 1)
    first_half = jnp.bitwise_and(lane, 31) < 16
    bd = bd_ref[...]

    def normed_rope(p, gain):
        ssq = _dot((p * p).astype(BF16), bd)
        y = p * lax.rsqrt(ssq * (1.0 / HEAD_DIM) + EPS)
        slabs = []
        for s in range(MXU_DIM // LANES):
            ys = y[:, s * LANES:(s + 1) * LANES] * gain
            rot = jnp.where(first_half, pltpu.roll(ys, LANES - 16, 1), pltpu.roll(ys, 16, 1))
            slabs.append(ys * cos + rot * sin)
        return slabs

    kg = kg_ref[...]
    ones = jnp.ones((ONES_ROWS, tm), BF16)

    def finish_k(p, c):
        for s, slab in enumerate(normed_rope(p, kg)):
            k_ref[0, 2 * c + s] = slab.astype(BF16)

    def finish_v(v, c):
        for s in range(MXU_DIM // LANES):
            hd = 2 * c + s
            vT_ref[0, hd, 0:V_HEAD_DIM, :] = v[:, s * LANES:(s + 1) * LANES].T.astype(BF16)
            vT_ref[0, hd, V_HEAD_DIM:VT_ROWS, :] = ones

    def finish_q(p, c):
        qg = qg_ref[...] * (HEAD_DIM ** -0.5 * LOG2E)
        for s, slab in enumerate(normed_rope(p, qg)):
            st = slab.T.astype(BF16)
            for j in range(tm // MXU_DIM):
                qT_ref[0, 2 * c + s, j] = st[:, j * MXU_DIM:(j + 1) * MXU_DIM]

    def finish_f(f, c):
        f_ref[0, :, c * MXU_DIM:(c + 1) * MXU_DIM] = f.astype(BF16)

    def finish_gate(g, c):
        gate_ref[0, :, c * MXU_DIM:(c + 1) * MXU_DIM] = _sigmoid(g).astype(BF16)

    sections = [(0, ATTN_WIDTH, finish_k), (ATTN_WIDTH, ATTN_WIDTH, finish_v)]
    if not kv_only:
        sections += [(Q_COL0, ATTN_WIDTH, finish_q), (F_COL0, FOURIER_WIDTH, finish_f),
                     (G_COL0, 2 * D_MODEL, finish_gate)]
    pending = None
    for col0, width, finish in sections:
        for c in range(width // MXU_DIM):
            p = _dot(hb, w_ref[:, col0 + c * MXU_DIM:col0 + (c + 1) * MXU_DIM])
            if pending is not None:
                pending[0](pending[1], pending[2])
            pending = (finish, p, c)
    pending[0](pending[1], pending[2])


def _inproj(x, mod, per_batch_mod, g1, w, layer, cos, sin, qg, kg, bd, *, tm, kv_only):
    b, t, d = x.shape
    nq = tm // MXU_DIM
    n_w = KV_COLS if kv_only else w.shape[2]
    mod_map = (lambda i, j: (i, 0, 0)) if per_batch_mod else (lambda i, j: (0, 0, 0))
    in_specs = [
        pl.BlockSpec((1, tm, d), lambda i, j: (i, j, 0)),
        pl.BlockSpec((1, 1, N_MOD * d), mod_map),
        _const_spec((1, d)),
        _layer_spec(d, n_w, layer),
        pl.BlockSpec((tm, LANES), lambda i, j: (j, 0)),
        pl.BlockSpec((tm, LANES), lambda i, j: (j, 0)),
        _const_spec((1, LANES)),
        _const_spec((1, LANES)),
        _const_spec((MXU_DIM, MXU_DIM)),
    ]
    k_spec = pl.BlockSpec((1, N_HEADS, tm, LANES), lambda i, j: (i, 0, j, 0))
    vT_spec = pl.BlockSpec((1, N_HEADS, VT_ROWS, tm), lambda i, j: (i, 0, 0, j))
    k_shape = jax.ShapeDtypeStruct((b, N_HEADS, t, LANES), BF16)
    vT_shape = jax.ShapeDtypeStruct((b, N_HEADS, VT_ROWS, t), BF16)
    if kv_only:
        out_specs = [k_spec, vT_spec]
        out_shape = [k_shape, vT_shape]
    else:
        out_specs = [
            pl.BlockSpec((1, N_HEADS, nq, LANES, MXU_DIM), lambda i, j: (i, 0, j, 0, 0)),
            k_spec, vT_spec,
            pl.BlockSpec((1, tm, FOURIER_WIDTH), lambda i, j: (i, j, 0)),
            pl.BlockSpec((1, tm, 2 * d), lambda i, j: (i, j, 0)),
        ]
        out_shape = [
            jax.ShapeDtypeStruct((b, N_HEADS, t // MXU_DIM, LANES, MXU_DIM), BF16),
            k_shape, vT_shape,
            jax.ShapeDtypeStruct((b, t, FOURIER_WIDTH), BF16),
            jax.ShapeDtypeStruct((b, t, 2 * d), BF16),
        ]
    return pl.pallas_call(
        functools.partial(_inproj_kernel, kv_only=kv_only),
        grid=(b, t // tm),
        in_specs=in_specs,
        out_specs=out_specs,
        out_shape=out_shape,
        compiler_params=_params(2),
        name="inproj_kv" if kv_only else "inproj",
    )(x, mod, g1, w, cos, sin, qg, kg, bd)


def _attn_kernel(*refs, has_lat, lam_init):
    if has_lat:
        qT_ref, kl_ref, kc_ref, vTl_ref, vTc_ref, lam_ref, sg_ref, o_ref = refs[:8]
        tl = kl_ref.shape[2]
    else:
        qT_ref, kc_ref, vTc_ref, lam_ref, sg_ref, o_ref = refs[:6]
        tl = 0
    m_ref, acc_ref = refs[-2 * ATTN_SLOTS - 2], refs[-2 * ATTN_SLOTS - 1]
    s_refs = refs[-2 * ATTN_SLOTS:-ATTN_SLOTS]
    p_refs = refs[-ATTN_SLOTS:]
    tc = kc_ref.shape[2]
    n_heads, nb = qT_ref.shape[1], qT_ref.shape[2]
    n_blk = n_heads * nb
    ch = MXU_DIM

    def split(j):
        if isinstance(j, int):
            return j // nb, j % nb
        assert nb & (nb - 1) == 0
        return lax.shift_right_logical(j, nb.bit_length() - 1), lax.bitwise_and(j, nb - 1)
    lp = lam_ref[...]
    lam = (jnp.exp(jnp.sum(lp[0:1] * lp[1:2], axis=-1, keepdims=True))
           - jnp.exp(jnp.sum(lp[2:3] * lp[3:4], axis=-1, keepdims=True)) + lam_init)
    out_gain = sg_ref[...] * (1.0 - lam_init)
    row = lax.broadcasted_iota(jnp.int32, (V_HEAD_DIM, MXU_DIM), 0)
    zero = jnp.zeros((V_HEAD_DIM, MXU_DIM), BF16)

    chunks = ([(c * ch, kl_ref, c * ch) for c in range(tl // ch)]
              + [(tl + c * ch, kc_ref, c * ch) for c in range(tc // ch)])

    def score_chunk(slot, w, chunk, head):
        dst, k_ref, src = chunk
        sc = jnp.dot(k_ref[0, head, src:src + ch, :], w, preferred_element_type=F32)
        s_refs[slot][dst:dst + ch, :] = sc
        return jnp.max(sc, axis=0, keepdims=True)

    def prob_chunk(slot, m, chunk):
        dst = chunk[0]
        p_refs[slot][dst:dst + ch, :] = jnp.exp2(s_refs[slot][dst:dst + ch, :] - m).astype(BF16)

    def weighted_values(slot, head):
        p_ref = p_refs[slot]
        a = jnp.dot(vTc_ref[0, head], p_ref[tl:, :], preferred_element_type=F32)
        if has_lat:
            a = a + jnp.dot(vTl_ref[0, head], p_ref[0:tl, :], preferred_element_type=F32)
        return a

    def finish_block(j, acc1, acc2):
        head, blk = split(j)
        o1 = acc1[0:V_HEAD_DIM] * (1.0 / acc1[V_HEAD_DIM:V_HEAD_DIM + 1])
        o2 = acc2[0:V_HEAD_DIM] * (1.0 / acc2[V_HEAD_DIM:V_HEAD_DIM + 1])
        o = o1 - lam * o2
        ms = jnp.sum(o * o, axis=0, keepdims=True) * (1.0 / V_HEAD_DIM)
        on = (o * lax.rsqrt(ms + EPS)).T * out_gain
        if isinstance(blk, int):
            rows = slice(blk * MXU_DIM, (blk + 1) * MXU_DIM)
        else:
            rows = pl.ds(pl.multiple_of(blk * MXU_DIM, MXU_DIM), MXU_DIM)
        o_ref[0, head, rows, :] = on.astype(BF16)

    grp = ATTN_BLOCKS_PER_STAGE
    assert n_blk % grp == 0
    n_units = n_blk // grp

    def slot(par, g, half):
        return (par * grp + g) * 2 + half

    def stage(u, par, do_finish, do_values, do_scores, do_probs):
        other = 1 - par
        gs = range(grp)
        if do_finish:
            for g in gs:
                finish_block(grp * (u - 2) + g, acc_ref[slot(par, g, 0)], acc_ref[slot(par, g, 1)])
        if do_values:
            v_head = [split(grp * (u - 1) + g)[0] for g in gs]
        if do_scores:
            s_head, w = [], []
            for g in gs:
                head, blk = split(grp * (u + 1) + g)
                qT = qT_ref[0, head, blk]
                s_head.append(head)
                w.append((jnp.where(row < HEAD_DIM, qT, zero), jnp.where(row >= HEAD_DIM, qT, zero)))
        if do_probs:
            pm = [[m_ref[slot(par, g, half), 0:1, :] for half in range(2)] for g in gs]
        mx = [[None, None] for _ in gs]
        for chunk in chunks:
            for g in gs:
                for half in range(2):
                    if do_scores:
                        c = score_chunk(slot(other, g, half), w[g][half], chunk, s_head[g])
                        mx[g][half] = c if mx[g][half] is None else jnp.maximum(mx[g][half], c)
            for g in gs:
                for half in range(2):
                    if do_probs:
                        prob_chunk(slot(par, g, half), pm[g][half], chunk)
        for g in gs:
            for half in range(2):
                if do_scores:
                    m_ref[slot(other, g, half)] = jnp.broadcast_to(mx[g][half], (8, MXU_DIM))
                if do_values:
                    acc_ref[slot(other, g, half)] = weighted_values(slot(other, g, half), v_head[g])

    def static_stage(u):
        stage(u, u % 2, do_finish=2 <= u <= n_units + 1, do_values=1 <= u <= n_units,
              do_scores=u + 1 < n_units, do_probs=0 <= u < n_units)

    full = range(2, n_units - 1)
    for j in range(-1, n_units + 2):
        if j not in full:
            static_stage(j)
        elif j == full[0]:
            def body(i, carry):
                @pl.when(i % 2 == 0)
                def _():
                    stage(i, 0, True, True, True, True)

                @pl.when(i % 2 == 1)
                def _():
                    stage(i, 1, True, True, True, True)
                return carry
            lax.fori_loop(full[0], full[-1] + 1, body, 0)


def _attention(qT, k_lat, k_ctx, vT_lat, vT_ctx, lam_rows, subln_g, lam_init):
    b, _, n_blk, _, _ = qT.shape
    tq = n_blk * MXU_DIM
    tc = k_ctx.shape[2]
    has_lat = k_lat is not None
    tl = k_lat.shape[2] if has_lat else 0
    hs = ATTN_HEADS_PER_STEP
    head5 = lambda i, h: (i, h, 0, 0, 0)
    head4 = lambda i, h: (i, h, 0, 0)
    in_specs = [pl.BlockSpec((1, hs, n_blk, LANES, MXU_DIM), head5)]
    args = [qT]
    if has_lat:
        in_specs += [pl.BlockSpec((1, hs, tl, LANES), head4), pl.BlockSpec((1, hs, tc, LANES), head4),
                     pl.BlockSpec((1, hs, VT_ROWS, tl), head4), pl.BlockSpec((1, hs, VT_ROWS, tc), head4)]
        args += [k_lat, k_ctx, vT_lat, vT_ctx]
    else:
        in_specs += [pl.BlockSpec((1, hs, tc, LANES), head4), pl.BlockSpec((1, hs, VT_ROWS, tc), head4)]
        args += [k_ctx, vT_ctx]
    in_specs += [_const_spec((4, HEAD_DIM)), _const_spec((1, V_HEAD_DIM))]
    args += [lam_rows, subln_g]
    return pl.pallas_call(
        functools.partial(_attn_kernel, has_lat=has_lat, lam_init=lam_init),
        grid=(b, N_HEADS // hs),
        in_specs=in_specs,
        out_specs=pl.BlockSpec((1, hs, tq, V_HEAD_DIM), head4),
        out_shape=jax.ShapeDtypeStruct((b, N_HEADS, tq, V_HEAD_DIM), BF16),
        scratch_shapes=([pltpu.VMEM((ATTN_SLOTS, 8, MXU_DIM), F32),
                         pltpu.VMEM((ATTN_SLOTS, VT_ROWS, MXU_DIM), F32)]
                        + [pltpu.VMEM((tl + tc, MXU_DIM), F32)] * ATTN_SLOTS
                        + [pltpu.VMEM((tl + tc, MXU_DIM), BF16)] * ATTN_SLOTS),
        compiler_params=_params(2),
        name="diff_attention" if has_lat else "diff_attention_ctx",
    )(*args)


def _fourier_kernel(f_ref, ch_ref, sh_ref, cs_ref, rev_ref, o_ref, xc_ref, xs_ref, y_ref, *, scale):
    t = f_ref.shape[1]
    half = t // 2
    cs = cs_ref[...]
    for g in range(N_FOURIER_GROUPS):
        cols = slice(g * FOURIER_GROUP_DIM, (g + 1) * FOURIER_GROUP_DIM)
        r = _dot(f_ref[0, :, cols], cs)
        xc_ref[:, cols] = r[:, 0:FOURIER_GROUP_DIM].astype(BF16)
        xs_ref[:, cols] = r[:, FOURIER_GROUP_DIM:].astype(BF16)
    n_rows = ch_ref.shape[0]
    tr = min(512, half)
    for r0 in range(0, n_rows, tr):
        rows = slice(r0, min(r0 + tr, n_rows))
        a = _dot(ch_ref[rows, :], xc_ref[...])
        b = _dot(sh_ref[rows, :], xs_ref[...])
        if r0 < half:
            o_ref[0, rows, :] = ((a - b) * scale).astype(BF16)
        y_ref[rows, :] = ((a + b) * scale).astype(BF16)
    rb = rev_ref.shape[0]
    win = rev_ref.shape[1]
    for blk in range(half // rb):
        src = half - rb * (blk + 1)
        o_ref[0, half + blk * rb:half + (blk + 1) * rb, :] = _dot(
            rev_ref[...], y_ref[src:src + win, :]).astype(BF16)


def _fourier(f, ch, sh, cs, rev):
    b, t, w = f.shape
    scale = 1.0 / math.sqrt(t * FOURIER_GROUP_DIM)
    n_rows = ch.shape[0]
    return pl.pallas_call(
        functools.partial(_fourier_kernel, scale=scale),
        grid=(b,),
        in_specs=[
            pl.BlockSpec((1, t, w), lambda i: (i, 0, 0)),
            _const_spec(ch.shape), _const_spec(sh.shape),
            _const_spec((FOURIER_GROUP_DIM, 2 * FOURIER_GROUP_DIM)),
            _const_spec(rev.shape),
        ],
        out_specs=pl.BlockSpec((1, t, w), lambda i: (i, 0, 0)),
        out_shape=jax.ShapeDtypeStruct((b, t, w), BF16),
        scratch_shapes=[pltpu.VMEM((t, w), BF16), pltpu.VMEM((t, w), BF16), pltpu.VMEM((n_rows, w), BF16)],
        compiler_params=_params(1),
        name="fourier_mix",
    )(f, ch, sh, cs, rev)


def _merge_ffn_kernel(o_ref, mix_ref, gate_ref, x_ref, mod_ref, g2_ref, wpa_ref, wpf_ref, wout_ref,
                      wgu_ref, wdn_ref, out_ref, act_ref):
    d = D_MODEL
    mod = mod_ref[0]
    tm = x_ref.shape[1]
    subs = [slice(r, r + MXU_DIM) for r in range(0, tm, MXU_DIM)]

    branches = []
    for rows in subs:
        o = jnp.concatenate([o_ref[0, hd, rows, :] for hd in range(N_HEADS)], axis=-1)
        branches.append((_dot(o, wpa_ref[...]), _dot(mix_ref[0, rows, :], wpf_ref[...])))
    projected = []
    for rows, (y_a, y_f) in zip(subs, branches):
        g_a = gate_ref[0, rows, 0:d].astype(F32)
        g_f = gate_ref[0, rows, d:2 * d].astype(F32)
        projected.append(_dot((g_a * y_a + g_f * y_f).astype(BF16), wout_ref[...]))
    x1s, h2s = [], []
    for rows, out in zip(subs, projected):
        x1 = x_ref[0, rows, :] + mod[:, 2 * d:3 * d] * out
        x1s.append(x1)
        h2s.append(_norm_modulate(x1, g2_ref[...], mod[:, 3 * d:4 * d], mod[:, 4 * d:5 * d]).astype(BF16))
    cw = MXU_DIM
    for c in range(D_FF // cw):
        for rows, h2 in zip(subs, h2s):
            gate = _dot(h2, wgu_ref[:, c * cw:(c + 1) * cw])
            up = _dot(h2, wgu_ref[:, D_FF + c * cw:D_FF + (c + 1) * cw])
            act_ref[rows, c * cw:(c + 1) * cw] = (gate * _sigmoid(gate) * up).astype(BF16)
    for rows, x1 in zip(subs, x1s):
        out_ref[0, rows, :] = x1 + mod[:, 5 * d:6 * d] * _dot(act_ref[rows, :], wdn_ref[...])


def _merge_ffn(o, mix, gates, x, mod, per_batch_mod, g2, layer, wpa, wpf, wout, wgu, wdn, *, tm):
    b, t, d = x.shape
    mod_map = (lambda i, j: (i, 0, 0)) if per_batch_mod else (lambda i, j: (0, 0, 0))
    tok = lambda i, j: (i, j, 0)
    return pl.pallas_call(
        _merge_ffn_kernel,
        grid=(b, t // tm),
        in_specs=[
            pl.BlockSpec((1, N_HEADS, tm, V_HEAD_DIM), lambda i, j: (i, 0, j, 0)),
            pl.BlockSpec((1, tm, FOURIER_WIDTH), tok),
            pl.BlockSpec((1, tm, 2 * d), tok),
            pl.BlockSpec((1, tm, d), tok),
            pl.BlockSpec((1, 1, N_MOD * d), mod_map),
            _const_spec((1, d)),
            *[_layer_spec(w.shape[1], w.shape[2], layer) for w in (wpa, wpf, wout, wgu, wdn)],
        ],
        out_specs=pl.BlockSpec((1, tm, d), tok),
        out_shape=jax.ShapeDtypeStruct((b, t, d), F32),
        scratch_shapes=[pltpu.VMEM((tm, D_FF), BF16)],
        compiler_params=_params(2),
        name="merge_ffn",
    )(o, mix, gates, x, mod, g2, wpa, wpf, wout, wgu, wdn)


def _rope_tables(n_tokens):
    rows = n_tokens // GRID_W
    r, col = np.meshgrid(np.arange(rows), np.arange(GRID_W), indexing="ij")
    r = r.reshape(-1).astype(np.float32)
    col = col.reshape(-1).astype(np.float32)
    axis_dim = HEAD_DIM // 2
    inv_freq = (ROPE_THETA ** (-np.arange(0, axis_dim, 2, dtype=np.float32) / axis_dim)).astype(np.float32)
    ang_r = r[:, None] * inv_freq[None, :]
    ang_c = col[:, None] * inv_freq[None, :]
    ang = np.concatenate([ang_r, ang_r, ang_c, ang_c], axis=-1)
    sign = np.where((np.arange(HEAD_DIM) % 32) < 16, -1.0, 1.0).astype(np.float32)
    cos = np.cos(ang).astype(np.float32)
    sin = (np.sin(ang) * sign).astype(np.float32)
    return np.tile(cos, (1, 2)), np.tile(sin, (1, 2))


BF16_SUBLANES = 16


def _dft_tables(n):
    half = n // 2
    rows = half + BF16_SUBLANES
    kt = (np.arange(half + 1)[:, None] * np.arange(n)[None, :]) % n
    ang = kt.astype(np.float64) * (2.0 * np.pi / n)
    cos = np.zeros((rows, n), np.float32)
    sin = np.zeros((rows, n), np.float32)
    cos[:half + 1] = np.cos(ang)
    sin[:half + 1] = np.sin(ang)
    return cos, sin


def _row_reversal(n):
    rb = min(MXU_DIM, n // 2)
    rev = np.zeros((rb, rb + BF16_SUBLANES), np.float32)
    rev[np.arange(rb), rb - np.arange(rb)] = 1.0
    return rev


def _channel_dft_table():
    n = FOURIER_GROUP_DIM
    kt = (np.arange(n)[:, None] * np.arange(n)[None, :]) % n
    ang = kt.astype(np.float64) * (2.0 * np.pi / n)
    return np.concatenate([np.cos(ang), np.sin(ang)], axis=1).astype(np.float32)


def _group_sum_matrix():
    g = np.arange(MXU_DIM) // HEAD_DIM
    return (g[:, None] == g[None, :]).astype(np.float32)


def kernel(x, c, ctx, c_ctx, w_ada, b_ada, norm1_g, norm2_g, w_in, q_norm_g, k_norm_g,
           lambda_q1, lambda_k1, lambda_q2, lambda_k2, subln_g, w_proj_attn, w_proj_fourier,
           w_out, w_gate_up, w_down):
    b, t, d = x.shape
    tc = ctx.shape[1]
    depth = w_ada.shape[0]
    assert b + 1 <= MOD_ROWS and d == D_MODEL

    cc = jnp.zeros((MOD_ROWS, d), F32).at[:b].set(c).at[b].set(c_ctx)
    mod = _modulation(cc, w_ada, b_ada)

    cos_np, sin_np = _rope_tables(t)
    cos_lat, sin_lat = jnp.asarray(cos_np), jnp.asarray(sin_np)
    cos_ctx = jnp.ones((tc, LANES), F32)
    sin_ctx = jnp.zeros((tc, LANES), F32)
    dft_lat = [jnp.asarray(a).astype(BF16) for a in (*_dft_tables(t), _row_reversal(t))]
    dft_ctx = [jnp.asarray(a).astype(BF16) for a in (*_dft_tables(tc), _row_reversal(tc))]
    cs = jnp.asarray(_channel_dft_table()).astype(BF16)
    bd = jnp.asarray(_group_sum_matrix()).astype(BF16)

    w_in_b = w_in.astype(BF16)
    weights = tuple(w.astype(BF16) for w in (w_proj_attn, w_proj_fourier, w_out, w_gate_up, w_down))

    cx = ctx
    for l in range(depth):
        last = l == depth - 1
        lam_init = 0.8 - 0.6 * math.exp(-0.3 * l)
        mod_lat = mod[l, :b][:, None, :]
        mod_ctx = mod[l, b:b + 1][None]
        g1 = norm1_g[l][None]
        g2 = norm2_g[l][None]
        qg = jnp.tile(q_norm_g[l], 2)[None]
        kg = jnp.tile(k_norm_g[l], 2)[None]
        lam_rows = jnp.stack([lambda_q1[l], lambda_k1[l], lambda_q2[l], lambda_k2[l]])
        sg = subln_g[l][None]

        if last:
            k_ctx, vT_ctx = _inproj(cx, mod_ctx, False, g1, w_in_b, l, cos_ctx, sin_ctx,
                                    qg, kg, bd, tm=tc, kv_only=True)
        else:
            qT_ctx, k_ctx, vT_ctx, f_ctx, gates_ctx = _inproj(
                cx, mod_ctx, False, g1, w_in_b, l, cos_ctx, sin_ctx, qg, kg, bd, tm=tc, kv_only=False)
        qT, k_lat, vT_lat, f_lat, gates = _inproj(
            x, mod_lat, True, g1, w_in_b, l, cos_lat, sin_lat, qg, kg, bd, tm=512, kv_only=False)

        o = _attention(qT, k_lat, k_ctx, vT_lat, vT_ctx, lam_rows, sg, lam_init)
        mix = _fourier(f_lat, dft_lat[0], dft_lat[1], cs, dft_lat[2])
        x = _merge_ffn(o, mix, gates, x, mod_lat, True, g2, l, *weights, tm=512)

        if not last:
            o_ctx = _attention(qT_ctx, None, k_ctx, None, vT_ctx, lam_rows, sg, lam_init)
            mix_ctx = _fourier(f_ctx, dft_ctx[0], dft_ctx[1], cs, dft_ctx[2])
            cx = _merge_ffn(o_ctx, mix_ctx, gates_ctx, cx, mod_ctx, False, g2, l, *weights, tm=tc)
    return x
```

```python
import functools
import math

import jax
import jax.numpy as jnp
import numpy as np
from jax import lax
from jax.experimental import pallas as pl
from jax.experimental.pallas import tpu as pltpu

F32 = jnp.float32
BF16 = jnp.bfloat16

D_MODEL = 1024
N_HEADS = 8
HEAD_DIM = 64
V_HEAD_DIM = 2 * HEAD_DIM
ATTN_WIDTH = N_HEADS * V_HEAD_DIM
N_FOURIER_GROUPS = 4
FOURIER_GROUP_DIM = 128
FOURIER_WIDTH = N_FOURIER_GROUPS * FOURIER_GROUP_DIM
KV_COLS = 2 * ATTN_WIDTH
Q_COL0 = KV_COLS
F_COL0 = 3 * ATTN_WIDTH
G_COL0 = F_COL0 + FOURIER_WIDTH
IN_WIDTH = G_COL0 + 2 * D_MODEL
D_FF = 2816
N_MOD = 6
GRID_W = 64
ROPE_THETA = 10000.0
EPS = 1e-6

LANES = 128
MXU_DIM = 256
BF16_SUBLANES = 16
ONES_ROWS = BF16_SUBLANES
ROPE_AXIS_DIM = HEAD_DIM // 2
ROPE_HALF = ROPE_AXIS_DIM // 2
VT_ROWS = V_HEAD_DIM + ONES_ROWS
MOD_ROWS = 24
VMEM_LIMIT = 56 * 1024 * 1024
LOG2E = 1.4426950408889634
ATTN_BLOCKS_PER_STAGE = 1
ATTN_SLOTS = 4 * ATTN_BLOCKS_PER_STAGE
ATTN_HEADS_PER_STEP = 8


def _params(n_axes):
    return pltpu.CompilerParams(dimension_semantics=("arbitrary",) * n_axes,
                                vmem_limit_bytes=VMEM_LIMIT)


def _const_spec(shape):
    nd = len(shape)
    return pl.BlockSpec(shape, lambda *_: (0,) * nd, pipeline_mode=pl.Buffered(1))


def _layer_spec(rows, cols, layer):
    return pl.BlockSpec((None, rows, cols), lambda *_: (layer, 0, 0), pipeline_mode=pl.Buffered(1))


def _dot(a, b):
    return jnp.dot(a, b, preferred_element_type=F32)


def _sigmoid(x):
    return 1.0 / (1.0 + jnp.exp(-x))


def _mod_kernel(c_ref, w_ref, b_ref, o_ref):
    c = c_ref[...]
    s = c * _sigmoid(c)
    s_hi = s.astype(BF16)
    s_lo = (s - s_hi.astype(F32)).astype(BF16)
    w = w_ref[0]
    w_hi = w.astype(BF16)
    w_lo = (w - w_hi.astype(F32)).astype(BF16)
    acc = _dot(s_hi, w_hi) + _dot(s_hi, w_lo) + _dot(s_lo, w_hi)
    o_ref[0] = acc + b_ref[0]


def _modulation(cc, w_ada, b_ada):
    depth, d, n = w_ada.shape
    tn = 2048
    return pl.pallas_call(
        _mod_kernel,
        grid=(depth, n // tn),
        in_specs=[
            pl.BlockSpec((MOD_ROWS, d), lambda l, j: (0, 0)),
            pl.BlockSpec((1, d, tn), lambda l, j: (l, 0, j)),
            pl.BlockSpec((1, 1, tn), lambda l, j: (l, 0, j)),
        ],
        out_specs=pl.BlockSpec((1, MOD_ROWS, tn), lambda l, j: (l, 0, j)),
        out_shape=jax.ShapeDtypeStruct((depth, MOD_ROWS, n), F32),
        compiler_params=_params(2),
        name="adaln_modulation",
    )(cc, w_ada, b_ada.reshape(depth, 1, n))


def _norm_modulate(x, g, shift, scale):
    ms = jnp.mean(x * x, axis=-1, keepdims=True)
    return (x * lax.rsqrt(ms + EPS) * g) * (1.0 + scale) + shift


def _inproj_kernel(x_ref, mod_ref, g1_ref, w_ref, cos_ref, sin_ref, qg_ref, kg_ref, bd_ref,
                   *out_refs, kv_only):
    if kv_only:
        k_ref, vT_ref = out_refs
    else:
        qT_ref, k_ref, vT_ref, f_ref, gate_ref = out_refs
    tm = x_ref.shape[1]
    mod = mod_ref[0]
    h = _norm_modulate(x_ref[0], g1_ref[...], mod[:, 0:D_MODEL], mod[:, D_MODEL:2 * D_MODEL])
    hb = h.astype(BF16)
    cos = cos_ref[...]
    sin = sin_ref[...]
    lane = lax.broadcasted_iota(jnp.int32, (tm, LANES), 1)
    first_half = jnp.bitwise_and(lane, ROPE_AXIS_DIM - 1) < ROPE_HALF
    bd = bd_ref[...]

    def normed_rope(p, gain):
        ssq = _dot((p * p).astype(BF16), bd)
        y = p * lax.rsqrt(ssq * (1.0 / HEAD_DIM) + EPS)
        slabs = []
        for s in range(MXU_DIM // LANES):
            ys = y[:, s * LANES:(s + 1) * LANES] * gain
            rot = jnp.where(first_half, pltpu.roll(ys, LANES - ROPE_HALF, 1), pltpu.roll(ys, ROPE_HALF, 1))
            slabs.append(ys * cos + rot * sin)
        return slabs

    kg = kg_ref[...]
    ones = jnp.ones((ONES_ROWS, tm), BF16)

    def finish_k(p, c):
        for s, slab in enumerate(normed_rope(p, kg)):
            k_ref[0, 2 * c + s] = slab.astype(BF16)

    def finish_v(v, c):
        for s in range(MXU_DIM // LANES):
            hd = 2 * c + s
            vT_ref[0, hd, 0:V_HEAD_DIM, :] = v[:, s * LANES:(s + 1) * LANES].T.astype(BF16)
            vT_ref[0, hd, V_HEAD_DIM:VT_ROWS, :] = ones

    def finish_q(p, c):
        qg = qg_ref[...] * (HEAD_DIM ** -0.5 * LOG2E)
        for s, slab in enumerate(normed_rope(p, qg)):
            st = slab.T.astype(BF16)
            for j in range(tm // MXU_DIM):
                qT_ref[0, 2 * c + s, j] = st[:, j * MXU_DIM:(j + 1) * MXU_DIM]

    def finish_f(f, c):
        f_ref[0, :, c * MXU_DIM:(c + 1) * MXU_DIM] = f.astype(BF16)

    def finish_gate(g, c):
        gate_ref[0, :, c * MXU_DIM:(c + 1) * MXU_DIM] = _sigmoid(g).astype(BF16)

    sections = [(0, ATTN_WIDTH, finish_k), (ATTN_WIDTH, ATTN_WIDTH, finish_v)]
    if not kv_only:
        sections += [(Q_COL0, ATTN_WIDTH, finish_q), (F_COL0, FOURIER_WIDTH, finish_f),
                     (G_COL0, 2 * D_MODEL, finish_gate)]
    pending = None
    for col0, width, finish in sections:
        for c in range(width // MXU_DIM):
            p = _dot(hb, w_ref[:, col0 + c * MXU_DIM:col0 + (c + 1) * MXU_DIM])
            if pending is not None:
                pending[0](pending[1], pending[2])
            pending = (finish, p, c)
    pending[0](pending[1], pending[2])


def _inproj(x, mod, per_batch_mod, g1, w, layer, cos, sin, qg, kg, bd, *, tm, kv_only):
    b, t, d = x.shape
    nq = tm // MXU_DIM
    n_w = KV_COLS if kv_only else w.shape[2]
    mod_map = (lambda i, j: (i, 0, 0)) if per_batch_mod else (lambda i, j: (0, 0, 0))
    in_specs = [
        pl.BlockSpec((1, tm, d), lambda i, j: (i, j, 0)),
        pl.BlockSpec((1, 1, N_MOD * d), mod_map),
        _const_spec((1, d)),
        _layer_spec(d, n_w, layer),
        pl.BlockSpec((tm, LANES), lambda i, j: (j, 0)),
        pl.BlockSpec((tm, LANES), lambda i, j: (j, 0)),
        _const_spec((1, LANES)),
        _const_spec((1, LANES)),
        _const_spec((MXU_DIM, MXU_DIM)),
    ]
    k_spec = pl.BlockSpec((1, N_HEADS, tm, LANES), lambda i, j: (i, 0, j, 0))
    vT_spec = pl.BlockSpec((1, N_HEADS, VT_ROWS, tm), lambda i, j: (i, 0, 0, j))
    k_shape = jax.ShapeDtypeStruct((b, N_HEADS, t, LANES), BF16)
    vT_shape = jax.ShapeDtypeStruct((b, N_HEADS, VT_ROWS, t), BF16)
    if kv_only:
        out_specs = [k_spec, vT_spec]
        out_shape = [k_shape, vT_shape]
    else:
        out_specs = [
            pl.BlockSpec((1, N_HEADS, nq, LANES, MXU_DIM), lambda i, j: (i, 0, j, 0, 0)),
            k_spec, vT_spec,
            pl.BlockSpec((1, tm, FOURIER_WIDTH), lambda i, j: (i, j, 0)),
            pl.BlockSpec((1, tm, 2 * d), lambda i, j: (i, j, 0)),
        ]
        out_shape = [
            jax.ShapeDtypeStruct((b, N_HEADS, t // MXU_DIM, LANES, MXU_DIM), BF16),
            k_shape, vT_shape,
            jax.ShapeDtypeStruct((b, t, FOURIER_WIDTH), BF16),
            jax.ShapeDtypeStruct((b, t, 2 * d), BF16),
        ]
    return pl.pallas_call(
        functools.partial(_inproj_kernel, kv_only=kv_only),
        grid=(b, t // tm),
        in_specs=in_specs,
        out_specs=out_specs,
        out_shape=out_shape,
        compiler_params=_params(2),
        name="inproj_kv" if kv_only else "inproj",
    )(x, mod, g1, w, cos, sin, qg, kg, bd)


def _attn_kernel(*refs, has_lat, lam_init):
    if has_lat:
        qT_ref, kl_ref, kc_ref, vTl_ref, vTc_ref, lam_ref, sg_ref, o_ref = refs[:8]
        tl = kl_ref.shape[2]
    else:
        qT_ref, kc_ref, vTc_ref, lam_ref, sg_ref, o_ref = refs[:6]
        tl = 0
    m_ref, acc_ref = refs[-2 * ATTN_SLOTS - 2], refs[-2 * ATTN_SLOTS - 1]
    s_refs = refs[-2 * ATTN_SLOTS:-ATTN_SLOTS]
    p_refs = refs[-ATTN_SLOTS:]
    tc = kc_ref.shape[2]
    n_heads, nb = qT_ref.shape[1], qT_ref.shape[2]
    n_blk = n_heads * nb
    ch = MXU_DIM

    def split(j):
        if isinstance(j, int):
            return j // nb, j % nb
        assert nb & (nb - 1) == 0
        return lax.shift_right_logical(j, nb.bit_length() - 1), lax.bitwise_and(j, nb - 1)
    lp = lam_ref[...]
    lam = (jnp.exp(jnp.sum(lp[0:1] * lp[1:2], axis=-1, keepdims=True))
           - jnp.exp(jnp.sum(lp[2:3] * lp[3:4], axis=-1, keepdims=True)) + lam_init)
    out_gain = sg_ref[...] * (1.0 - lam_init)
    row = lax.broadcasted_iota(jnp.int32, (V_HEAD_DIM, MXU_DIM), 0)
    zero = jnp.zeros((V_HEAD_DIM, MXU_DIM), BF16)

    chunks = ([(c * ch, kl_ref, c * ch) for c in range(tl // ch)]
              + [(tl + c * ch, kc_ref, c * ch) for c in range(tc // ch)])

    def score_chunk(slot, w, chunk, head):
        dst, k_ref, src = chunk
        sc = jnp.dot(k_ref[0, head, src:src + ch, :], w, preferred_element_type=F32)
        s_refs[slot][dst:dst + ch, :] = sc
        return jnp.max(sc, axis=0, keepdims=True)

    def prob_chunk(slot, m, chunk):
        dst = chunk[0]
        p_refs[slot][dst:dst + ch, :] = jnp.exp2(s_refs[slot][dst:dst + ch, :] - m).astype(BF16)

    def weighted_values(slot, head):
        p_ref = p_refs[slot]
        a = jnp.dot(vTc_ref[0, head], p_ref[tl:, :], preferred_element_type=F32)
        if has_lat:
            a = a + jnp.dot(vTl_ref[0, head], p_ref[0:tl, :], preferred_element_type=F32)
        return a

    def finish_block(j, acc1, acc2):
        head, blk = split(j)
        o1 = acc1[0:V_HEAD_DIM] * (1.0 / acc1[V_HEAD_DIM:V_HEAD_DIM + 1])
        o2 = acc2[0:V_HEAD_DIM] * (1.0 / acc2[V_HEAD_DIM:V_HEAD_DIM + 1])
        o = o1 - lam * o2
        ms = jnp.sum(o * o, axis=0, keepdims=True) * (1.0 / V_HEAD_DIM)
        on = (o * lax.rsqrt(ms + EPS)).T * out_gain
        if isinstance(blk, int):
            rows = slice(blk * MXU_DIM, (blk + 1) * MXU_DIM)
        else:
            rows = pl.ds(pl.multiple_of(blk * MXU_DIM, MXU_DIM), MXU_DIM)
        o_ref[0, head, rows, :] = on.astype(BF16)

    grp = ATTN_BLOCKS_PER_STAGE
    assert n_blk % grp == 0
    n_units = n_blk // grp

    def slot(par, g, half):
        return (par * grp + g) * 2 + half

    def stage(u, par, do_finish, do_values, do_scores, do_probs):
        other = 1 - par
        gs = range(grp)
        if do_finish:
            for g in gs:
                finish_block(grp * (u - 2) + g, acc_ref[slot(par, g, 0)], acc_ref[slot(par, g, 1)])
        if do_values:
            v_head = [split(grp * (u - 1) + g)[0] for g in gs]
        if do_scores:
            s_head, w = [], []
            for g in gs:
                head, blk = split(grp * (u + 1) + g)
                qT = qT_ref[0, head, blk]
                s_head.append(head)
                w.append((jnp.where(row < HEAD_DIM, qT, zero), jnp.where(row >= HEAD_DIM, qT, zero)))
        if do_probs:
            pm = [[m_ref[slot(par, g, half), 0:1, :] for half in range(2)] for g in gs]
        mx = [[None, None] for _ in gs]
        for chunk in chunks:
            for g in gs:
                for half in range(2):
                    if do_scores:
                        c = score_chunk(slot(other, g, half), w[g][half], chunk, s_head[g])
                        mx[g][half] = c if mx[g][half] is None else jnp.maximum(mx[g][half], c)
            for g in gs:
                for half in range(2):
                    if do_probs:
                        prob_chunk(slot(par, g, half), pm[g][half], chunk)
        for g in gs:
            for half in range(2):
                if do_scores:
                    m_ref[slot(other, g, half)] = jnp.broadcast_to(mx[g][half], (8, MXU_DIM))
                if do_values:
                    acc_ref[slot(other, g, half)] = weighted_values(slot(other, g, half), v_head[g])

    def static_stage(u):
        stage(u, u % 2, do_finish=2 <= u <= n_units + 1, do_values=1 <= u <= n_units,
              do_scores=u + 1 < n_units, do_probs=0 <= u < n_units)

    full = range(2, n_units - 1)
    for j in range(-1, n_units + 2):
        if j not in full:
            static_stage(j)
        elif j == full[0]:
            def body(i, carry):
                @pl.when(i % 2 == 0)
                def _():
                    stage(i, 0, True, True, True, True)

                @pl.when(i % 2 == 1)
                def _():
                    stage(i, 1, True, True, True, True)
                return carry
            lax.fori_loop(full[0], full[-1] + 1, body, 0)


def _attention(qT, k_lat, k_ctx, vT_lat, vT_ctx, lam_rows, subln_g, lam_init):
    b, _, n_blk, _, _ = qT.shape
    tq = n_blk * MXU_DIM
    tc = k_ctx.shape[2]
    has_lat = k_lat is not None
    tl = k_lat.shape[2] if has_lat else 0
    hs = ATTN_HEADS_PER_STEP
    head5 = lambda i, h: (i, h, 0, 0, 0)
    head4 = lambda i, h: (i, h, 0, 0)
    in_specs = [pl.BlockSpec((1, hs, n_blk, LANES, MXU_DIM), head5)]
    args = [qT]
    if has_lat:
        in_specs += [pl.BlockSpec((1, hs, tl, LANES), head4), pl.BlockSpec((1, hs, tc, LANES), head4),
                     pl.BlockSpec((1, hs, VT_ROWS, tl), head4), pl.BlockSpec((1, hs, VT_ROWS, tc), head4)]
        args += [k_lat, k_ctx, vT_lat, vT_ctx]
    else:
        in_specs += [pl.BlockSpec((1, hs, tc, LANES), head4), pl.BlockSpec((1, hs, VT_ROWS, tc), head4)]
        args += [k_ctx, vT_ctx]
    in_specs += [_const_spec((4, HEAD_DIM)), _const_spec((1, V_HEAD_DIM))]
    args += [lam_rows, subln_g]
    return pl.pallas_call(
        functools.partial(_attn_kernel, has_lat=has_lat, lam_init=lam_init),
        grid=(b, N_HEADS // hs),
        in_specs=in_specs,
        out_specs=pl.BlockSpec((1, hs, tq, V_HEAD_DIM), head4),
        out_shape=jax.ShapeDtypeStruct((b, N_HEADS, tq, V_HEAD_DIM), BF16),
        scratch_shapes=([pltpu.VMEM((ATTN_SLOTS, 8, MXU_DIM), F32),
                         pltpu.VMEM((ATTN_SLOTS, VT_ROWS, MXU_DIM), F32)]
                        + [pltpu.VMEM((tl + tc, MXU_DIM), F32)] * ATTN_SLOTS
                        + [pltpu.VMEM((tl + tc, MXU_DIM), BF16)] * ATTN_SLOTS),
        compiler_params=_params(2),
        name="diff_attention" if has_lat else "diff_attention_ctx",
    )(*args)


def _fourier_kernel(f_ref, ch_ref, sh_ref, cs_ref, rev_ref, o_ref, xc_ref, xs_ref, y_ref, *, scale):
    t = f_ref.shape[1]
    half = t // 2
    cs = cs_ref[...]
    for g in range(N_FOURIER_GROUPS):
        cols = slice(g * FOURIER_GROUP_DIM, (g + 1) * FOURIER_GROUP_DIM)
        r = _dot(f_ref[0, :, cols], cs)
        xc_ref[:, cols] = r[:, 0:FOURIER_GROUP_DIM].astype(BF16)
        xs_ref[:, cols] = r[:, FOURIER_GROUP_DIM:].astype(BF16)
    n_rows = ch_ref.shape[0]
    tr = min(512, half)
    for r0 in range(0, n_rows, tr):
        rows = slice(r0, min(r0 + tr, n_rows))
        a = _dot(ch_ref[rows, :], xc_ref[...])
        b = _dot(sh_ref[rows, :], xs_ref[...])
        if r0 < half:
            o_ref[0, rows, :] = ((a - b) * scale).astype(BF16)
        y_ref[rows, :] = ((a + b) * scale).astype(BF16)
    rb = rev_ref.shape[0]
    win = rev_ref.shape[1]
    for blk in range(half // rb):
        src = half - rb * (blk + 1)
        o_ref[0, half + blk * rb:half + (blk + 1) * rb, :] = _dot(
            rev_ref[...], y_ref[src:src + win, :]).astype(BF16)


def _fourier(f, ch, sh, cs, rev):
    b, t, w = f.shape
    scale = 1.0 / math.sqrt(t * FOURIER_GROUP_DIM)
    n_rows = ch.shape[0]
    return pl.pallas_call(
        functools.partial(_fourier_kernel, scale=scale),
        grid=(b,),
        in_specs=[
            pl.BlockSpec((1, t, w), lambda i: (i, 0, 0)),
            _const_spec(ch.shape), _const_spec(sh.shape),
            _const_spec((FOURIER_GROUP_DIM, 2 * FOURIER_GROUP_DIM)),
            _const_spec(rev.shape),
        ],
        out_specs=pl.BlockSpec((1, t, w), lambda i: (i, 0, 0)),
        out_shape=jax.ShapeDtypeStruct((b, t, w), BF16),
        scratch_shapes=[pltpu.VMEM((t, w), BF16), pltpu.VMEM((t, w), BF16), pltpu.VMEM((n_rows, w), BF16)],
        compiler_params=_params(1),
        name="fourier_mix",
    )(f, ch, sh, cs, rev)


def _merge_ffn_kernel(o_ref, mix_ref, gate_ref, x_ref, mod_ref, g2_ref, wpa_ref, wpf_ref, wout_ref,
                      wgu_ref, wdn_ref, out_ref, act_ref):
    d = D_MODEL
    mod = mod_ref[0]
    tm = x_ref.shape[1]
    subs = [slice(r, r + MXU_DIM) for r in range(0, tm, MXU_DIM)]

    branches = []
    for rows in subs:
        o = jnp.concatenate([o_ref[0, hd, rows, :] for hd in range(N_HEADS)], axis=-1)
        branches.append((_dot(o, wpa_ref[...]), _dot(mix_ref[0, rows, :], wpf_ref[...])))
    projected = []
    for rows, (y_a, y_f) in zip(subs, branches):
        g_a = gate_ref[0, rows, 0:d].astype(F32)
        g_f = gate_ref[0, rows, d:2 * d].astype(F32)
        projected.append(_dot((g_a * y_a + g_f * y_f).astype(BF16), wout_ref[...]))
    x1s, h2s = [], []
    for rows, out in zip(subs, projected):
        x1 = x_ref[0, rows, :] + mod[:, 2 * d:3 * d] * out
        x1s.append(x1)
        h2s.append(_norm_modulate(x1, g2_ref[...], mod[:, 3 * d:4 * d], mod[:, 4 * d:5 * d]).astype(BF16))
    cw = MXU_DIM
    for c in range(D_FF // cw):
        for rows, h2 in zip(subs, h2s):
            gate = _dot(h2, wgu_ref[:, c * cw:(c + 1) * cw])
            up = _dot(h2, wgu_ref[:, D_FF + c * cw:D_FF + (c + 1) * cw])
            act_ref[rows, c * cw:(c + 1) * cw] = (gate * _sigmoid(gate) * up).astype(BF16)
    for rows, x1 in zip(subs, x1s):
        out_ref[0, rows, :] = x1 + mod[:, 5 * d:6 * d] * _dot(act_ref[rows, :], wdn_ref[...])


def _merge_ffn(o, mix, gates, x, mod, per_batch_mod, g2, layer, wpa, wpf, wout, wgu, wdn, *, tm):
    b, t, d = x.shape
    mod_map = (lambda i, j: (i, 0, 0)) if per_batch_mod else (lambda i, j: (0, 0, 0))
    tok = lambda i, j: (i, j, 0)
    return pl.pallas_call(
        _merge_ffn_kernel,
        grid=(b, t // tm),
        in_specs=[
            pl.BlockSpec((1, N_HEADS, tm, V_HEAD_DIM), lambda i, j: (i, 0, j, 0)),
            pl.BlockSpec((1, tm, FOURIER_WIDTH), tok),
            pl.BlockSpec((1, tm, 2 * d), tok),
            pl.BlockSpec((1, tm, d), tok),
            pl.BlockSpec((1, 1, N_MOD * d), mod_map),
            _const_spec((1, d)),
            *[_layer_spec(w.shape[1], w.shape[2], layer) for w in (wpa, wpf, wout, wgu, wdn)],
        ],
        out_specs=pl.BlockSpec((1, tm, d), tok),
        out_shape=jax.ShapeDtypeStruct((b, t, d), F32),
        scratch_shapes=[pltpu.VMEM((tm, D_FF), BF16)],
        compiler_params=_params(2),
        name="merge_ffn",
    )(o, mix, gates, x, mod, g2, wpa, wpf, wout, wgu, wdn)


def _rope_tables(n_tokens):
    rows = n_tokens // GRID_W
    r, col = np.meshgrid(np.arange(rows), np.arange(GRID_W), indexing="ij")
    r = r.reshape(-1).astype(np.float32)
    col = col.reshape(-1).astype(np.float32)
    axis_dim = HEAD_DIM // 2
    inv_freq = (ROPE_THETA ** (-np.arange(0, axis_dim, 2, dtype=np.float32) / axis_dim)).astype(np.float32)
    ang_r = r[:, None] * inv_freq[None, :]
    ang_c = col[:, None] * inv_freq[None, :]
    ang = np.concatenate([ang_r, ang_r, ang_c, ang_c], axis=-1)
    sign = np.where((np.arange(HEAD_DIM) % ROPE_AXIS_DIM) < ROPE_HALF, -1.0, 1.0).astype(np.float32)
    cos = np.cos(ang).astype(np.float32)
    sin = (np.sin(ang) * sign).astype(np.float32)
    return np.tile(cos, (1, 2)), np.tile(sin, (1, 2))


def _dft_tables(n):
    half = n // 2
    rows = half + BF16_SUBLANES
    kt = (np.arange(half + 1)[:, None] * np.arange(n)[None, :]) % n
    ang = kt.astype(np.float64) * (2.0 * np.pi / n)
    cos = np.zeros((rows, n), np.float32)
    sin = np.zeros((rows, n), np.float32)
    cos[:half + 1] = np.cos(ang)
    sin[:half + 1] = np.sin(ang)
    return cos, sin


def _row_reversal(n):
    rb = min(MXU_DIM, n // 2)
    rev = np.zeros((rb, rb + BF16_SUBLANES), np.float32)
    rev[np.arange(rb), rb - np.arange(rb)] = 1.0
    return rev


def _channel_dft_table():
    n = FOURIER_GROUP_DIM
    kt = (np.arange(n)[:, None] * np.arange(n)[None, :]) % n
    ang = kt.astype(np.float64) * (2.0 * np.pi / n)
    return np.concatenate([np.cos(ang), np.sin(ang)], axis=1).astype(np.float32)


def _group_sum_matrix():
    g = np.arange(MXU_DIM) // HEAD_DIM
    return (g[:, None] == g[None, :]).astype(np.float32)


def kernel(x, c, ctx, c_ctx, w_ada, b_ada, norm1_g, norm2_g, w_in, q_norm_g, k_norm_g,
           lambda_q1, lambda_k1, lambda_q2, lambda_k2, subln_g, w_proj_attn, w_proj_fourier,
           w_out, w_gate_up, w_down):
    b, t, d = x.shape
    tc = ctx.shape[1]
    depth = w_ada.shape[0]
    assert b + 1 <= MOD_ROWS and d == D_MODEL

    cc = jnp.zeros((MOD_ROWS, d), F32).at[:b].set(c).at[b].set(c_ctx)
    mod = _modulation(cc, w_ada, b_ada)

    cos_np, sin_np = _rope_tables(t)
    cos_lat, sin_lat = jnp.asarray(cos_np), jnp.asarray(sin_np)
    cos_ctx = jnp.ones((tc, LANES), F32)
    sin_ctx = jnp.zeros((tc, LANES), F32)
    dft_lat = [jnp.asarray(a).astype(BF16) for a in (*_dft_tables(t), _row_reversal(t))]
    dft_ctx = [jnp.asarray(a).astype(BF16) for a in (*_dft_tables(tc), _row_reversal(tc))]
    cs = jnp.asarray(_channel_dft_table()).astype(BF16)
    bd = jnp.asarray(_group_sum_matrix()).astype(BF16)

    w_in_b = w_in.astype(BF16)
    weights = tuple(w.astype(BF16) for w in (w_proj_attn, w_proj_fourier, w_out, w_gate_up, w_down))

    cx = ctx
    for l in range(depth):
        last = l == depth - 1
        lam_init = 0.8 - 0.6 * math.exp(-0.3 * l)
        mod_lat = mod[l, :b][:, None, :]
        mod_ctx = mod[l, b:b + 1][None]
        g1 = norm1_g[l][None]
        g2 = norm2_g[l][None]
        qg = jnp.tile(q_norm_g[l], 2)[None]
        kg = jnp.tile(k_norm_g[l], 2)[None]
        lam_rows = jnp.stack([lambda_q1[l], lambda_k1[l], lambda_q2[l], lambda_k2[l]])
        sg = subln_g[l][None]

        if last:
            k_ctx, vT_ctx = _inproj(cx, mod_ctx, False, g1, w_in_b, l, cos_ctx, sin_ctx,
                                    qg, kg, bd, tm=tc, kv_only=True)
        else:
            qT_ctx, k_ctx, vT_ctx, f_ctx, gates_ctx = _inproj(
                cx, mod_ctx, False, g1, w_in_b, l, cos_ctx, sin_ctx, qg, kg, bd, tm=tc, kv_only=False)
        qT, k_lat, vT_lat, f_lat, gates = _inproj(
            x, mod_lat, True, g1, w_in_b, l, cos_lat, sin_lat, qg, kg, bd, tm=512, kv_only=False)

        o = _attention(qT, k_lat, k_ctx, vT_lat, vT_ctx, lam_rows, sg, lam_init)
        mix = _fourier(f_lat, dft_lat[0], dft_lat[1], cs, dft_lat[2])
        x = _merge_ffn(o, mix, gates, x, mod_lat, True, g2, l, *weights, tm=512)

        if not last:
            o_ctx = _attention(qT_ctx, None, k_ctx, None, vT_ctx, lam_rows, sg, lam_init)
            mix_ctx = _fourier(f_ctx, dft_ctx[0], dft_ctx[1], cs, dft_ctx[2])
            cx = _merge_ffn(o_ctx, mix_ctx, gates_ctx, cx, mod_ctx, False, g2, l, *weights, tm=tc)
    return x
```

```python
import functools
import math

import jax
import jax.numpy as jnp
import numpy as np
from jax import lax
from jax.experimental import pallas as pl
from jax.experimental.pallas import tpu as pltpu

F32 = jnp.float32
BF16 = jnp.bfloat16

D_MODEL = 1024
N_HEADS = 8
HEAD_DIM = 64
V_HEAD_DIM = 2 * HEAD_DIM
ATTN_WIDTH = N_HEADS * V_HEAD_DIM
N_FOURIER_GROUPS = 4
FOURIER_GROUP_DIM = 128
FOURIER_WIDTH = N_FOURIER_GROUPS * FOURIER_GROUP_DIM
KV_COLS = 2 * ATTN_WIDTH
Q_COL0 = KV_COLS
F_COL0 = 3 * ATTN_WIDTH
G_COL0 = F_COL0 + FOURIER_WIDTH
IN_WIDTH = G_COL0 + 2 * D_MODEL
D_FF = 2816
N_MOD = 6
GRID_W = 64
ROPE_THETA = 10000.0
EPS = 1e-6

LANES = 128
MXU_DIM = 256
BF16_SUBLANES = 16
ONES_ROWS = BF16_SUBLANES
ROPE_AXIS_DIM = HEAD_DIM // 2
ROPE_HALF = ROPE_AXIS_DIM // 2
VT_ROWS = V_HEAD_DIM + ONES_ROWS
MOD_ROWS = 24
VMEM_LIMIT = 56 * 1024 * 1024
LOG2E = 1.4426950408889634
ATTN_BLOCKS_PER_STAGE = 1
ATTN_SLOTS = 4 * ATTN_BLOCKS_PER_STAGE
ATTN_HEADS_PER_STEP = 8


def _params(n_axes):
    return pltpu.CompilerParams(dimension_semantics=("arbitrary",) * n_axes,
                                vmem_limit_bytes=VMEM_LIMIT)


def _const_spec(shape):
    nd = len(shape)
    return pl.BlockSpec(shape, lambda *_: (0,) * nd, pipeline_mode=pl.Buffered(1))


def _layer_spec(rows, cols, layer):
    return pl.BlockSpec((None, rows, cols), lambda *_: (layer, 0, 0), pipeline_mode=pl.Buffered(1))


def _dot(a, b):
    return jnp.dot(a, b, preferred_element_type=F32)


def _sigmoid(x):
    return 1.0 / (1.0 + jnp.exp(-x))


def _mod_kernel(c_ref, w_ref, b_ref, o_ref):
    c = c_ref[...]
    s = c * _sigmoid(c)
    s_hi = s.astype(BF16)
    s_lo = (s - s_hi.astype(F32)).astype(BF16)
    w = w_ref[0]
    w_hi = w.astype(BF16)
    w_lo = (w - w_hi.astype(F32)).astype(BF16)
    acc = _dot(s_hi, w_hi) + _dot(s_hi, w_lo) + _dot(s_lo, w_hi)
    o_ref[0] = acc + b_ref[0]


def _modulation(cc, w_ada, b_ada):
    depth, d, n = w_ada.shape
    tn = 2048
    return pl.pallas_call(
        _mod_kernel,
        grid=(depth, n // tn),
        in_specs=[
            pl.BlockSpec((MOD_ROWS, d), lambda l, j: (0, 0)),
            pl.BlockSpec((1, d, tn), lambda l, j: (l, 0, j)),
            pl.BlockSpec((1, 1, tn), lambda l, j: (l, 0, j)),
        ],
        out_specs=pl.BlockSpec((1, MOD_ROWS, tn), lambda l, j: (l, 0, j)),
        out_shape=jax.ShapeDtypeStruct((depth, MOD_ROWS, n), F32),
        compiler_params=_params(2),
        name="adaln_modulation",
    )(cc, w_ada, b_ada.reshape(depth, 1, n))


def _norm_modulate(x, g, shift, scale):
    ms = jnp.mean(x * x, axis=-1, keepdims=True)
    return (x * lax.rsqrt(ms + EPS) * g) * (1.0 + scale) + shift


def _inproj_kernel(x_ref, mod_ref, g1_ref, w_ref, cos_ref, sin_ref, qg_ref, kg_ref, bd_ref,
                   *out_refs, kv_only):
    if kv_only:
        k_ref, vT_ref = out_refs
    else:
        qT_ref, k_ref, vT_ref, f_ref, gate_ref = out_refs
    tm = x_ref.shape[1]
    mod = mod_ref[0]
    h = _norm_modulate(x_ref[0], g1_ref[...], mod[:, 0:D_MODEL], mod[:, D_MODEL:2 * D_MODEL])
    hb = h.astype(BF16)
    cos = cos_ref[...]
    sin = sin_ref[...]
    lane = lax.broadcasted_iota(jnp.int32, (tm, LANES), 1)
    first_half = jnp.bitwise_and(lane, ROPE_AXIS_DIM - 1) < ROPE_HALF
    bd = bd_ref[...]

    def normed_rope(p, gain):
        ssq = _dot((p * p).astype(BF16), bd)
        y = p * lax.rsqrt(ssq * (1.0 / HEAD_DIM) + EPS)
        slabs = []
        for s in range(MXU_DIM // LANES):
            ys = y[:, s * LANES:(s + 1) * LANES] * gain
            rot = jnp.where(first_half, pltpu.roll(ys, LANES - ROPE_HALF, 1), pltpu.roll(ys, ROPE_HALF, 1))
            slabs.append(ys * cos + rot * sin)
        return slabs

    kg = kg_ref[...]
    ones = jnp.ones((ONES_ROWS, tm), BF16)

    def finish_k(p, c):
        for s, slab in enumerate(normed_rope(p, kg)):
            k_ref[0, 2 * c + s] = slab.astype(BF16)

    def finish_v(v, c):
        for s in range(MXU_DIM // LANES):
            hd = 2 * c + s
            vT_ref[0, hd, 0:V_HEAD_DIM, :] = v[:, s * LANES:(s + 1) * LANES].T.astype(BF16)
            vT_ref[0, hd, V_HEAD_DIM:VT_ROWS, :] = ones

    def finish_q(p, c):
        qg = qg_ref[...] * (HEAD_DIM ** -0.5 * LOG2E)
        for s, slab in enumerate(normed_rope(p, qg)):
            st = slab.T.astype(BF16)
            for j in range(tm // MXU_DIM):
                qT_ref[0, 2 * c + s, j] = st[:, j * MXU_DIM:(j + 1) * MXU_DIM]

    def finish_f(f, c):
        f_ref[0, :, c * MXU_DIM:(c + 1) * MXU_DIM] = f.astype(BF16)

    def finish_gate(g, c):
        gate_ref[0, :, c * MXU_DIM:(c + 1) * MXU_DIM] = _sigmoid(g).astype(BF16)

    sections = [(0, ATTN_WIDTH, finish_k), (ATTN_WIDTH, ATTN_WIDTH, finish_v)]
    if not kv_only:
        sections += [(Q_COL0, ATTN_WIDTH, finish_q), (F_COL0, FOURIER_WIDTH, finish_f),
                     (G_COL0, 2 * D_MODEL, finish_gate)]
    pending = None
    for col0, width, finish in sections:
        for c in range(width // MXU_DIM):
            p = _dot(hb, w_ref[:, col0 + c * MXU_DIM:col0 + (c + 1) * MXU_DIM])
            if pending is not None:
                pending[0](pending[1], pending[2])
            pending = (finish, p, c)
    pending[0](pending[1], pending[2])


def _inproj(x, mod, per_batch_mod, g1, w, layer, cos, sin, qg, kg, bd, *, tm, kv_only):
    b, t, d = x.shape
    nq = tm // MXU_DIM
    n_w = KV_COLS if kv_only else w.shape[2]
    mod_map = (lambda i, j: (i, 0, 0)) if per_batch_mod else (lambda i, j: (0, 0, 0))
    in_specs = [
        pl.BlockSpec((1, tm, d), lambda i, j: (i, j, 0)),
        pl.BlockSpec((1, 1, N_MOD * d), mod_map),
        _const_spec((1, d)),
        _layer_spec(d, n_w, layer),
        pl.BlockSpec((tm, LANES), lambda i, j: (j, 0)),
        pl.BlockSpec((tm, LANES), lambda i, j: (j, 0)),
        _const_spec((1, LANES)),
        _const_spec((1, LANES)),
        _const_spec((MXU_DIM, MXU_DIM)),
    ]
    k_spec = pl.BlockSpec((1, N_HEADS, tm, LANES), lambda i, j: (i, 0, j, 0))
    vT_spec = pl.BlockSpec((1, N_HEADS, VT_ROWS, tm), lambda i, j: (i, 0, 0, j))
    k_shape = jax.ShapeDtypeStruct((b, N_HEADS, t, LANES), BF16)
    vT_shape = jax.ShapeDtypeStruct((b, N_HEADS, VT_ROWS, t), BF16)
    if kv_only:
        out_specs = [k_spec, vT_spec]
        out_shape = [k_shape, vT_shape]
    else:
        out_specs = [
            pl.BlockSpec((1, N_HEADS, nq, LANES, MXU_DIM), lambda i, j: (i, 0, j, 0, 0)),
            k_spec, vT_spec,
            pl.BlockSpec((1, tm, FOURIER_WIDTH), lambda i, j: (i, j, 0)),
            pl.BlockSpec((1, tm, 2 * d), lambda i, j: (i, j, 0)),
        ]
        out_shape = [
            jax.ShapeDtypeStruct((b, N_HEADS, t // MXU_DIM, LANES, MXU_DIM), BF16),
            k_shape, vT_shape,
            jax.ShapeDtypeStruct((b, t, FOURIER_WIDTH), BF16),
            jax.ShapeDtypeStruct((b, t, 2 * d), BF16),
        ]
    return pl.pallas_call(
        functools.partial(_inproj_kernel, kv_only=kv_only),
        grid=(b, t // tm),
        in_specs=in_specs,
        out_specs=out_specs,
        out_shape=out_shape,
        compiler_params=_params(2),
        name="inproj_kv" if kv_only else "inproj",
    )(x, mod, g1, w, cos, sin, qg, kg, bd)


def _attn_kernel(*refs, has_lat, lam_init):
    if has_lat:
        qT_ref, kl_ref, kc_ref, vTl_ref, vTc_ref, lam_ref, sg_ref, o_ref = refs[:8]
        tl = kl_ref.shape[2]
    else:
        qT_ref, kc_ref, vTc_ref, lam_ref, sg_ref, o_ref = refs[:6]
        tl = 0
    m_ref, acc_ref = refs[-ATTN_SLOTS - 2], refs[-ATTN_SLOTS - 1]
    p_refs = refs[-ATTN_SLOTS:]
    tc = kc_ref.shape[2]
    n_heads, nb = qT_ref.shape[1], qT_ref.shape[2]
    n_blk = n_heads * nb
    ch = MXU_DIM

    def split(j):
        if isinstance(j, int):
            return j // nb, j % nb
        assert nb & (nb - 1) == 0
        return lax.shift_right_logical(j, nb.bit_length() - 1), lax.bitwise_and(j, nb - 1)
    lp = lam_ref[...]
    lam = (jnp.exp(jnp.sum(lp[0:1] * lp[1:2], axis=-1, keepdims=True))
           - jnp.exp(jnp.sum(lp[2:3] * lp[3:4], axis=-1, keepdims=True)) + lam_init)
    out_gain = sg_ref[...] * (1.0 - lam_init)
    row = lax.broadcasted_iota(jnp.int32, (V_HEAD_DIM, MXU_DIM), 0)
    zero = jnp.zeros((V_HEAD_DIM, MXU_DIM), BF16)

    chunks = ([(c * ch, kl_ref, c * ch) for c in range(tl // ch)]
              + [(tl + c * ch, kc_ref, c * ch) for c in range(tc // ch)])

    def score_prob_chunk(slot, w, ci, chunk, head):
        dst, k_ref, src = chunk
        sc = jnp.dot(k_ref[0, head, src:src + ch, :], w, preferred_element_type=F32)
        mc = jnp.max(sc, axis=0, keepdims=True)
        p_refs[slot][dst:dst + ch, :] = jnp.exp2(sc - mc).astype(BF16)
        m_ref[slot, ci] = jnp.broadcast_to(mc, (8, MXU_DIM))

    def weighted_values(slot, head):
        chunk_max = [m_ref[slot, ci, 0:1, :] for ci in range(len(chunks))]
        total_max = functools.reduce(jnp.maximum, chunk_max)
        acc = None
        for ci, (dst, _, src) in enumerate(chunks):
            vT_ref = vTl_ref if dst < tl else vTc_ref
            part = jnp.dot(vT_ref[0, head, :, src:src + ch], p_refs[slot][dst:dst + ch, :],
                           preferred_element_type=F32)
            part = part * jnp.exp2(chunk_max[ci] - total_max)
            acc = part if acc is None else acc + part
        return acc

    def finish_block(j, acc1, acc2):
        head, blk = split(j)
        o1 = acc1[0:V_HEAD_DIM] * (1.0 / acc1[V_HEAD_DIM:V_HEAD_DIM + 1])
        o2 = acc2[0:V_HEAD_DIM] * (1.0 / acc2[V_HEAD_DIM:V_HEAD_DIM + 1])
        o = o1 - lam * o2
        ms = jnp.sum(o * o, axis=0, keepdims=True) * (1.0 / V_HEAD_DIM)
        on = (o * lax.rsqrt(ms + EPS)).T * out_gain
        if isinstance(blk, int):
            rows = slice(blk * MXU_DIM, (blk + 1) * MXU_DIM)
        else:
            rows = pl.ds(pl.multiple_of(blk * MXU_DIM, MXU_DIM), MXU_DIM)
        o_ref[0, head, rows, :] = on.astype(BF16)

    grp = ATTN_BLOCKS_PER_STAGE
    assert n_blk % grp == 0
    n_units = n_blk // grp

    def slot(par, g, half):
        return (par * grp + g) * 2 + half

    def stage(u, par, do_finish, do_values, do_scores):
        other = 1 - par
        gs = range(grp)
        if do_finish:
            for g in gs:
                finish_block(grp * (u - 1) + g, acc_ref[slot(other, g, 0)], acc_ref[slot(other, g, 1)])
        if do_scores:
            for g in gs:
                head, blk = split(grp * (u + 1) + g)
                qT = qT_ref[0, head, blk]
                w = (jnp.where(row < HEAD_DIM, qT, zero), jnp.where(row >= HEAD_DIM, qT, zero))
                for ci, chunk in enumerate(chunks):
                    for half in range(2):
                        score_prob_chunk(slot(other, g, half), w[half], ci, chunk, head)
        if do_values:
            for g in gs:
                v_head = split(grp * u + g)[0]
                for half in range(2):
                    acc_ref[slot(par, g, half)] = weighted_values(slot(par, g, half), v_head)

    def static_stage(u):
        stage(u, u % 2, do_finish=1 <= u <= n_units, do_values=0 <= u < n_units, do_scores=u + 1 < n_units)

    full = range(1, n_units - 1)
    for j in range(-1, n_units + 1):
        if j not in full:
            static_stage(j)
        elif j == full[0]:
            def body(i, carry):
                @pl.when(i % 2 == 0)
                def _():
                    stage(i, 0, True, True, True)

                @pl.when(i % 2 == 1)
                def _():
                    stage(i, 1, True, True, True)
                return carry
            lax.fori_loop(full[0], full[-1] + 1, body, 0)


def _attention(qT, k_lat, k_ctx, vT_lat, vT_ctx, lam_rows, subln_g, lam_init):
    b, _, n_blk, _, _ = qT.shape
    tq = n_blk * MXU_DIM
    tc = k_ctx.shape[2]
    has_lat = k_lat is not None
    tl = k_lat.shape[2] if has_lat else 0
    hs = ATTN_HEADS_PER_STEP
    head5 = lambda i, h: (i, h, 0, 0, 0)
    head4 = lambda i, h: (i, h, 0, 0)
    in_specs = [pl.BlockSpec((1, hs, n_blk, LANES, MXU_DIM), head5)]
    args = [qT]
    if has_lat:
        in_specs += [pl.BlockSpec((1, hs, tl, LANES), head4), pl.BlockSpec((1, hs, tc, LANES), head4),
                     pl.BlockSpec((1, hs, VT_ROWS, tl), head4), pl.BlockSpec((1, hs, VT_ROWS, tc), head4)]
        args += [k_lat, k_ctx, vT_lat, vT_ctx]
    else:
        in_specs += [pl.BlockSpec((1, hs, tc, LANES), head4), pl.BlockSpec((1, hs, VT_ROWS, tc), head4)]
        args += [k_ctx, vT_ctx]
    in_specs += [_const_spec((4, HEAD_DIM)), _const_spec((1, V_HEAD_DIM))]
    args += [lam_rows, subln_g]
    return pl.pallas_call(
        functools.partial(_attn_kernel, has_lat=has_lat, lam_init=lam_init),
        grid=(b, N_HEADS // hs),
        in_specs=in_specs,
        out_specs=pl.BlockSpec((1, hs, tq, V_HEAD_DIM), head4),
        out_shape=jax.ShapeDtypeStruct((b, N_HEADS, tq, V_HEAD_DIM), BF16),
        scratch_shapes=([pltpu.VMEM((ATTN_SLOTS, (tl + tc) // MXU_DIM, 8, MXU_DIM), F32),
                         pltpu.VMEM((ATTN_SLOTS, VT_ROWS, MXU_DIM), F32)]
                        + [pltpu.VMEM((tl + tc, MXU_DIM), BF16)] * ATTN_SLOTS),
        compiler_params=_params(2),
        name="diff_attention" if has_lat else "diff_attention_ctx",
    )(*args)


def _fourier_kernel(f_ref, ch_ref, sh_ref, cs_ref, rev_ref, o_ref, xc_ref, xs_ref, y_ref, *, scale):
    t = f_ref.shape[1]
    half = t // 2
    cs = cs_ref[...]
    for g in range(N_FOURIER_GROUPS):
        cols = slice(g * FOURIER_GROUP_DIM, (g + 1) * FOURIER_GROUP_DIM)
        r = _dot(f_ref[0, :, cols], cs)
        xc_ref[:, cols] = r[:, 0:FOURIER_GROUP_DIM].astype(BF16)
        xs_ref[:, cols] = r[:, FOURIER_GROUP_DIM:].astype(BF16)
    n_rows = ch_ref.shape[0]
    tr = min(512, half)
    for r0 in range(0, n_rows, tr):
        rows = slice(r0, min(r0 + tr, n_rows))
        a = _dot(ch_ref[rows, :], xc_ref[...])
        b = _dot(sh_ref[rows, :], xs_ref[...])
        if r0 < half:
            o_ref[0, rows, :] = ((a - b) * scale).astype(BF16)
        y_ref[rows, :] = ((a + b) * scale).astype(BF16)
    rb = rev_ref.shape[0]
    win = rev_ref.shape[1]
    for blk in range(half // rb):
        src = half - rb * (blk + 1)
        o_ref[0, half + blk * rb:half + (blk + 1) * rb, :] = _dot(
            rev_ref[...], y_ref[src:src + win, :]).astype(BF16)


def _fourier(f, ch, sh, cs, rev):
    b, t, w = f.shape
    scale = 1.0 / math.sqrt(t * FOURIER_GROUP_DIM)
    n_rows = ch.shape[0]
    return pl.pallas_call(
        functools.partial(_fourier_kernel, scale=scale),
        grid=(b,),
        in_specs=[
            pl.BlockSpec((1, t, w), lambda i: (i, 0, 0)),
            _const_spec(ch.shape), _const_spec(sh.shape),
            _const_spec((FOURIER_GROUP_DIM, 2 * FOURIER_GROUP_DIM)),
            _const_spec(rev.shape),
        ],
        out_specs=pl.BlockSpec((1, t, w), lambda i: (i, 0, 0)),
        out_shape=jax.ShapeDtypeStruct((b, t, w), BF16),
        scratch_shapes=[pltpu.VMEM((t, w), BF16), pltpu.VMEM((t, w), BF16), pltpu.VMEM((n_rows, w), BF16)],
        compiler_params=_params(1),
        name="fourier_mix",
    )(f, ch, sh, cs, rev)


def _merge_ffn_kernel(o_ref, mix_ref, gate_ref, x_ref, mod_ref, g2_ref, wpa_ref, wpf_ref, wout_ref,
                      wgu_ref, wdn_ref, out_ref, act_ref):
    d = D_MODEL
    mod = mod_ref[0]
    tm = x_ref.shape[1]
    subs = [slice(r, r + MXU_DIM) for r in range(0, tm, MXU_DIM)]

    branches = []
    for rows in subs:
        o = jnp.concatenate([o_ref[0, hd, rows, :] for hd in range(N_HEADS)], axis=-1)
        branches.append((_dot(o, wpa_ref[...]), _dot(mix_ref[0, rows, :], wpf_ref[...])))
    projected = []
    for rows, (y_a, y_f) in zip(subs, branches):
        g_a = gate_ref[0, rows, 0:d].astype(F32)
        g_f = gate_ref[0, rows, d:2 * d].astype(F32)
        projected.append(_dot((g_a * y_a + g_f * y_f).astype(BF16), wout_ref[...]))
    x1s, h2s = [], []
    for rows, out in zip(subs, projected):
        x1 = x_ref[0, rows, :] + mod[:, 2 * d:3 * d] * out
        x1s.append(x1)
        h2s.append(_norm_modulate(x1, g2_ref[...], mod[:, 3 * d:4 * d], mod[:, 4 * d:5 * d]).astype(BF16))
    cw = MXU_DIM
    for c in range(D_FF // cw):
        for rows, h2 in zip(subs, h2s):
            gate = _dot(h2, wgu_ref[:, c * cw:(c + 1) * cw])
            up = _dot(h2, wgu_ref[:, D_FF + c * cw:D_FF + (c + 1) * cw])
            act_ref[rows, c * cw:(c + 1) * cw] = (gate * _sigmoid(gate) * up).astype(BF16)
    for rows, x1 in zip(subs, x1s):
        out_ref[0, rows, :] = x1 + mod[:, 5 * d:6 * d] * _dot(act_ref[rows, :], wdn_ref[...])


def _merge_ffn(o, mix, gates, x, mod, per_batch_mod, g2, layer, wpa, wpf, wout, wgu, wdn, *, tm):
    b, t, d = x.shape
    mod_map = (lambda i, j: (i, 0, 0)) if per_batch_mod else (lambda i, j: (0, 0, 0))
    tok = lambda i, j: (i, j, 0)
    return pl.pallas_call(
        _merge_ffn_kernel,
        grid=(b, t // tm),
        in_specs=[
            pl.BlockSpec((1, N_HEADS, tm, V_HEAD_DIM), lambda i, j: (i, 0, j, 0)),
            pl.BlockSpec((1, tm, FOURIER_WIDTH), tok),
            pl.BlockSpec((1, tm, 2 * d), tok),
            pl.BlockSpec((1, tm, d), tok),
            pl.BlockSpec((1, 1, N_MOD * d), mod_map),
            _const_spec((1, d)),
            *[_layer_spec(w.shape[1], w.shape[2], layer) for w in (wpa, wpf, wout, wgu, wdn)],
        ],
        out_specs=pl.BlockSpec((1, tm, d), tok),
        out_shape=jax.ShapeDtypeStruct((b, t, d), F32),
        scratch_shapes=[pltpu.VMEM((tm, D_FF), BF16)],
        compiler_params=_params(2),
        name="merge_ffn",
    )(o, mix, gates, x, mod, g2, wpa, wpf, wout, wgu, wdn)


def _rope_tables(n_tokens):
    rows = n_tokens // GRID_W
    r, col = np.meshgrid(np.arange(rows), np.arange(GRID_W), indexing="ij")
    r = r.reshape(-1).astype(np.float32)
    col = col.reshape(-1).astype(np.float32)
    axis_dim = HEAD_DIM // 2
    inv_freq = (ROPE_THETA ** (-np.arange(0, axis_dim, 2, dtype=np.float32) / axis_dim)).astype(np.float32)
    ang_r = r[:, None] * inv_freq[None, :]
    ang_c = col[:, None] * inv_freq[None, :]
    ang = np.concatenate([ang_r, ang_r, ang_c, ang_c], axis=-1)
    sign = np.where((np.arange(HEAD_DIM) % ROPE_AXIS_DIM) < ROPE_HALF, -1.0, 1.0).astype(np.float32)
    cos = np.cos(ang).astype(np.float32)
    sin = (np.sin(ang) * sign).astype(np.float32)
    return np.tile(cos, (1, 2)), np.tile(sin, (1, 2))


def _dft_tables(n):
    half = n // 2
    rows = half + BF16_SUBLANES
    kt = (np.arange(half + 1)[:, None] * np.arange(n)[None, :]) % n
    ang = kt.astype(np.float64) * (2.0 * np.pi / n)
    cos = np.zeros((rows, n), np.float32)
    sin = np.zeros((rows, n), np.float32)
    cos[:half + 1] = np.cos(ang)
    sin[:half + 1] = np.sin(ang)
    return cos, sin


def _row_reversal(n):
    rb = min(MXU_DIM, n // 2)
    rev = np.zeros((rb, rb + BF16_SUBLANES), np.float32)
    rev[np.arange(rb), rb - np.arange(rb)] = 1.0
    return rev


def _channel_dft_table():
    n = FOURIER_GROUP_DIM
    kt = (np.arange(n)[:, None] * np.arange(n)[None, :]) % n
    ang = kt.astype(np.float64) * (2.0 * np.pi / n)
    return np.concatenate([np.cos(ang), np.sin(ang)], axis=1).astype(np.float32)


def _group_sum_matrix():
    g = np.arange(MXU_DIM) // HEAD_DIM
    return (g[:, None] == g[None, :]).astype(np.float32)


def kernel(x, c, ctx, c_ctx, w_ada, b_ada, norm1_g, norm2_g, w_in, q_norm_g, k_norm_g,
           lambda_q1, lambda_k1, lambda_q2, lambda_k2, subln_g, w_proj_attn, w_proj_fourier,
           w_out, w_gate_up, w_down):
    b, t, d = x.shape
    tc = ctx.shape[1]
    depth = w_ada.shape[0]
    assert b + 1 <= MOD_ROWS and d == D_MODEL

    cc = jnp.zeros((MOD_ROWS, d), F32).at[:b].set(c).at[b].set(c_ctx)
    mod = _modulation(cc, w_ada, b_ada)

    cos_np, sin_np = _rope_tables(t)
    cos_lat, sin_lat = jnp.asarray(cos_np), jnp.asarray(sin_np)
    cos_ctx = jnp.ones((tc, LANES), F32)
    sin_ctx = jnp.zeros((tc, LANES), F32)
    dft_lat = [jnp.asarray(a).astype(BF16) for a in (*_dft_tables(t), _row_reversal(t))]
    dft_ctx = [jnp.asarray(a).astype(BF16) for a in (*_dft_tables(tc), _row_reversal(tc))]
    cs = jnp.asarray(_channel_dft_table()).astype(BF16)
    bd = jnp.asarray(_group_sum_matrix()).astype(BF16)

    w_in_b = w_in.astype(BF16)
    weights = tuple(w.astype(BF16) for w in (w_proj_attn, w_proj_fourier, w_out, w_gate_up, w_down))

    cx = ctx
    for l in range(depth):
        last = l == depth - 1
        lam_init = 0.8 - 0.6 * math.exp(-0.3 * l)
        mod_lat = mod[l, :b][:, None, :]
        mod_ctx = mod[l, b:b + 1][None]
        g1 = norm1_g[l][None]
        g2 = norm2_g[l][None]
        qg = jnp.tile(q_norm_g[l], 2)[None]
        kg = jnp.tile(k_norm_g[l], 2)[None]
        lam_rows = jnp.stack([lambda_q1[l], lambda_k1[l], lambda_q2[l], lambda_k2[l]])
        sg = subln_g[l][None]

        if last:
            k_ctx, vT_ctx = _inproj(cx, mod_ctx, False, g1, w_in_b, l, cos_ctx, sin_ctx,
                                    qg, kg, bd, tm=tc, kv_only=True)
        else:
            qT_ctx, k_ctx, vT_ctx, f_ctx, gates_ctx = _inproj(
                cx, mod_ctx, False, g1, w_in_b, l, cos_ctx, sin_ctx, qg, kg, bd, tm=tc, kv_only=False)
        qT, k_lat, vT_lat, f_lat, gates = _inproj(
            x, mod_lat, True, g1, w_in_b, l, cos_lat, sin_lat, qg, kg, bd, tm=512, kv_only=False)

        o = _attention(qT, k_lat, k_ctx, vT_lat, vT_ctx, lam_rows, sg, lam_init)
        mix = _fourier(f_lat, dft_lat[0], dft_lat[1], cs, dft_lat[2])
        x = _merge_ffn(o, mix, gates, x, mod_lat, True, g2, l, *weights, tm=512)

        if not last:
            o_ctx = _attention(qT_ctx, None, k_ctx, None, vT_ctx, lam_rows, sg, lam_init)
            mix_ctx = _fourier(f_ctx, dft_ctx[0], dft_ctx[1], cs, dft_ctx[2])
            cx = _merge_ffn(o_ctx, mix_ctx, gates_ctx, cx, mod_ctx, False, g2, l, *weights, tm=tc)
    return x
```
